```python
import math
import jax, jax.numpy as jnp
from jax import lax
import numpy as np

D_MODEL = 4096
BATCH = 4
SEQ = 2048
DEPTH = 2
DEC_BATCH = 8
DEC_SEQ = 8
PAST_LEN = 16384
PAGE_SIZE = 128

D_MIX = D_MODEL // 2
N_BRANCH = 3
SSD_HEADDIM = 64
SSD_HEADS = D_MIX // SSD_HEADDIM
SSD_GROUPS = 4
SSD_STATE = 128
SSD_CONV = 4
SSD_CHUNK = 128
SSD_XBC = D_MIX + 2 * SSD_GROUPS * SSD_STATE
CONV_WIDTH = 31
CONV_CH = D_MIX
ATT_HEAD_DIM = 128
ATT_HEADS = D_MIX // ATT_HEAD_DIM
ATT_KV_HEADS = 4
MOBA_BLOCK = 256
MOBA_TOPK = 3
Q_CHUNK = 8
ROPE_THETA = 10000.0
D_FF = 4 * D_MODEL
EPS = 1e-6
N_IN = (D_MIX + SSD_XBC + SSD_HEADS) + 2 * CONV_CH + (ATT_HEADS + 2 * ATT_KV_HEADS) * ATT_HEAD_DIM + N_BRANCH * D_MODEL

kernel_name = 'hybrid_ssd_conformer_moba_step'


def _rmsnorm(x, g):
    xf = x.astype(jnp.float32)
    y = xf * lax.rsqrt(jnp.mean(xf * xf, axis=-1, keepdims=True) + EPS)
    return (y * g.astype(jnp.float32)).astype(x.dtype)


def _layernorm(x, g, b):
    xf = x.astype(jnp.float32)
    mu = jnp.mean(xf, axis=-1, keepdims=True)
    var = jnp.mean(jnp.square(xf - mu), axis=-1, keepdims=True)
    y = (xf - mu) * lax.rsqrt(var + EPS)
    return (y * g.astype(jnp.float32) + b.astype(jnp.float32)).astype(x.dtype)


def _causal_dwconv(u, buf, w, b):
    width = w.shape[0]
    ext = jnp.concatenate([buf.astype(u.dtype), u], axis=1)
    out = lax.conv_general_dilated(ext, w[:, None, :].astype(ext.dtype), window_strides=(1,),
                                   padding='VALID', dimension_numbers=('NWC', 'WIO', 'NWC'),
                                   feature_group_count=u.shape[-1])
    return out + b.astype(out.dtype), ext[:, ext.shape[1] - (width - 1):]


def _rope(x, pos):
    half = x.shape[-1] // 2
    inv_freq = jnp.exp(-math.log(ROPE_THETA) * jnp.arange(half, dtype=jnp.float32) * 2.0 / x.shape[-1])
    ang = pos[:, None] * inv_freq[None, :]
    cos = jnp.cos(ang)[:, None, :]
    sin = jnp.sin(ang)[:, None, :]
    xf = x.astype(jnp.float32)
    x1, x2 = xf[..., :half], xf[..., half:]
    return jnp.concatenate([x1 * cos - x2 * sin, x2 * cos + x1 * sin], axis=-1).astype(x.dtype)


def _ssd(x, dt, a, bm, cm, h0):
    b_, l_ = x.shape[:2]
    t_c = SSD_CHUNK if l_ % SSD_CHUNK == 0 else l_
    nc = l_ // t_c
    r = SSD_HEADS // SSD_GROUPS
    xc = x.astype(jnp.float32).reshape(b_, nc, t_c, SSD_GROUPS, r, SSD_HEADDIM)
    dtc = dt.reshape(b_, nc, t_c, SSD_GROUPS, r)
    bc = bm.astype(jnp.float32).reshape(b_, nc, t_c, SSD_GROUPS, SSD_STATE)
    cc = cm.astype(jnp.float32).reshape(b_, nc, t_c, SSD_GROUPS, SSD_STATE)
    acum = jnp.cumsum(dtc * a.reshape(SSD_GROUPS, r), axis=2)
    seg = acum[:, :, :, None] - acum[:, :, None, :]
    causal = jnp.tril(jnp.ones((t_c, t_c), dtype=bool))[:, :, None, None]
    decay = jnp.exp(jnp.where(causal, seg, -jnp.inf))
    cb = jnp.einsum('bctgn,bcsgn->bctsg', cc, bc)
    wts = cb[..., None] * decay * dtc[:, :, None]
    y_diag = jnp.einsum('bctsgr,bcsgrp->bctgrp', wts, xc)
    to_end = jnp.exp(acum[:, :, -1:] - acum) * dtc
    states = jnp.einsum('bctgn,bctgr,bctgrp->bcgrpn', bc, to_end, xc)
    chunk_decay = jnp.exp(acum[:, :, -1])

    def step(h, inp):
        st, dec = inp
        return h * dec[..., None, None] + st, h

    h_init = h0.astype(jnp.float32).reshape(b_, SSD_GROUPS, r, SSD_HEADDIM, SSD_STATE)
    h_fin, h_in = lax.scan(step, h_init, (jnp.moveaxis(states, 1, 0), jnp.moveaxis(chunk_decay, 1, 0)))
    h_in = jnp.moveaxis(h_in, 0, 1)
    y_off = jnp.einsum('bctgn,bcgrpn->bctgrp', cc, h_in) * jnp.exp(acum)[..., None]
    y = (y_diag + y_off).reshape(b_, l_, SSD_HEADS, SSD_HEADDIM)
    return y, h_fin.reshape(b_, SSD_HEADS, SSD_HEADDIM, SSD_STATE)


def _moba(q, k_all, v_all, qpos):
    b, t_len = q.shape[:2]
    lk = k_all.shape[1]
    nb = -(-lk // MOBA_BLOCK)
    pad = nb * MOBA_BLOCK - lk
    padw = ((0, 0), (0, pad), (0, 0), (0, 0))
    kb = jnp.pad(k_all, padw).reshape(b, nb, MOBA_BLOCK, ATT_KV_HEADS, ATT_HEAD_DIM).transpose(0, 3, 1, 2, 4)
    vb = jnp.pad(v_all, padw).reshape(b, nb, MOBA_BLOCK, ATT_KV_HEADS, ATT_HEAD_DIM).transpose(0, 3, 1, 2, 4)
    head_kv = jnp.arange(ATT_HEADS) // (ATT_HEADS // ATT_KV_HEADS)
    kmean = jnp.mean(kb.astype(jnp.float32), axis=3)[:, head_kv]
    n_sel = min(MOBA_TOPK, nb)
    qc = Q_CHUNK if t_len % Q_CHUNK == 0 else t_len
    nc = t_len // qc
    scale = ATT_HEAD_DIM ** -0.5
    bi = jnp.arange(b)[:, None, None, None]
    hi = head_kv[None, :, None, None]
    offs = jnp.arange(MOBA_BLOCK, dtype=jnp.int32)
    blk_ids = jnp.arange(nb, dtype=jnp.int32)

    def chunk(args):
        qq, pp = args
        qf = qq.astype(jnp.float32)
        qblk = pp // MOBA_BLOCK
        gate = jnp.einsum('bhqd,bhnd->bhqn', qf, kmean)
        gate = jnp.where(blk_ids[None, :] < qblk[:, None], gate, -jnp.inf)
        top_val, top_idx = lax.top_k(gate, n_sel)
        own = jnp.broadcast_to(qblk[None, None, :, None], top_idx.shape[:3] + (1,)).astype(top_idx.dtype)
        idx = jnp.concatenate([top_idx, own], axis=-1)
        slot_ok = jnp.concatenate([top_val > -jnp.inf, jnp.ones(own.shape, dtype=bool)], axis=-1)
        kg = kb[bi, hi, idx]
        vg = vb[bi, hi, idx]
        kpos = idx[..., None] * MOBA_BLOCK + offs
        mask = slot_ok[..., None] & (kpos <= pp[None, None, :, None, None])
        s = jnp.einsum('bhqd,bhqskd->bhqsk', qf, kg.astype(jnp.float32)) * scale
        s = jnp.where(mask, s, -jnp.inf).reshape(b, ATT_HEADS, qq.shape[2], -1)
        pr = jax.nn.softmax(s, axis=-1).reshape(mask.shape)
        return jnp.einsum('bhqsk,bhqskd->bhqd', pr, vg.astype(jnp.float32))

    qs = jnp.moveaxis(q.transpose(0, 2, 1, 3).reshape(b, ATT_HEADS, nc, qc, ATT_HEAD_DIM), 2, 0)
    ps = qpos.reshape(nc, qc)
    out = lax.map(chunk, (qs, ps))
    out = jnp.moveaxis(out, 0, 2).reshape(b, ATT_HEADS, t_len, ATT_HEAD_DIM)
    return out.transpose(0, 2, 1, 3).reshape(b, t_len, ATT_HEADS * ATT_HEAD_DIM)


def _layer(x, pos0, k_past, v_past, ssm_h0, ssm_buf0, conv_buf0, p):
    b, t_len, _ = x.shape
    h = _rmsnorm(x, p['norm_mix'])
    widths = (D_MIX, SSD_XBC, SSD_HEADS, 2 * CONV_CH, ATT_HEADS * ATT_HEAD_DIM,
              ATT_KV_HEADS * ATT_HEAD_DIM, ATT_KV_HEADS * ATT_HEAD_DIM, N_BRANCH * D_MODEL)
    cuts = [sum(widths[:i + 1]) for i in range(len(widths) - 1)]
    z, xbc, dt_raw, conv_in, q, k, v, gates = jnp.split(h @ p['w_in'], cuts, axis=-1)

    xbc, ssm_buf = _causal_dwconv(xbc, ssm_buf0, p['ssd_conv_w'], p['ssd_conv_b'])
    xbc = jax.nn.silu(xbc)
    nbc = SSD_GROUPS * SSD_STATE
    xs = xbc[..., :D_MIX].reshape(b, t_len, SSD_HEADS, SSD_HEADDIM)
    bm = xbc[..., D_MIX:D_MIX + nbc].reshape(b, t_len, SSD_GROUPS, SSD_STATE)
    cm = xbc[..., D_MIX + nbc:].reshape(b, t_len, SSD_GROUPS, SSD_STATE)
    dt = jax.nn.softplus(dt_raw.astype(jnp.float32) + p['ssd_dt_bias'].astype(jnp.float32))
    a = -jnp.exp(p['ssd_a_log'].astype(jnp.float32))
    y, ssm_h = _ssd(xs, dt, a, bm, cm, ssm_h0)
    y = y + p['ssd_d'].astype(jnp.float32)[:, None] * xs.astype(jnp.float32)
    y = y.reshape(b, t_len, D_MIX) * jax.nn.silu(z.astype(jnp.float32))
    yg = y.reshape(b, t_len, SSD_GROUPS, D_MIX // SSD_GROUPS)
    yg = yg * lax.rsqrt(jnp.mean(yg * yg, axis=-1, keepdims=True) + EPS)
    y = yg.reshape(b, t_len, D_MIX) * p['ssd_norm'].astype(jnp.float32)
    y_ssd = y.astype(x.dtype) @ p['w_ssd_out']

    u_val, u_gate = jnp.split(conv_in, 2, axis=-1)
    u = u_val * jax.nn.sigmoid(u_gate)
    u, conv_buf = _causal_dwconv(u, conv_buf0, p['conv_w'], p['conv_b'])
    u = jax.nn.silu(_layernorm(u, p['conv_ln_g'], p['conv_ln_b']))
    y_conv = u @ p['w_conv_out']

    posf = pos0 + jnp.arange(t_len, dtype=jnp.float32)
    qpos = pos0 + jnp.arange(t_len, dtype=jnp.int32)
    q = _rope(q.reshape(b, t_len, ATT_HEADS, ATT_HEAD_DIM), posf)
    k = _rope(k.reshape(b, t_len, ATT_KV_HEADS, ATT_HEAD_DIM), posf)
    v = v.reshape(b, t_len, ATT_KV_HEADS, ATT_HEAD_DIM)
    att = _moba(q, jnp.concatenate([k_past.astype(k.dtype), k], axis=1),
                jnp.concatenate([v_past.astype(v.dtype), v], axis=1), qpos)
    y_att = att.astype(x.dtype) @ p['w_att_out']

    g = jax.nn.sigmoid(gates.astype(jnp.float32)).astype(x.dtype)
    g_ssd, g_conv, g_att = jnp.split(g, N_BRANCH, axis=-1)
    x = x + (g_ssd * y_ssd + g_conv * y_conv + g_att * y_att) @ p['w_o']
    hf = _rmsnorm(x, p['norm_ffn'])
    x = x + jnp.square(jax.nn.relu(hf @ p['w_up'])) @ p['w_down']
    return x, (k, v, ssm_h, ssm_buf, conv_buf)


def setup_inputs(seed: int = 0) -> dict:
    key = jax.random.key(seed)
    ks = jax.random.split(key, 32)
    n_pages = PAST_LEN // PAGE_SIZE
    n_phys = (DEC_BATCH * n_pages * 5) // 4

    def nrm(k, shape, s):
        return jax.random.normal(k, shape, jnp.float32) * s

    dt0 = jnp.exp(jax.random.uniform(ks[10], (DEPTH, SSD_HEADS), jnp.float32,
                                     minval=math.log(1e-3), maxval=math.log(1e-1)))
    page_table = jax.random.permutation(ks[4], n_phys)[:DEC_BATCH * n_pages].reshape(DEC_BATCH, n_pages).astype(jnp.int32)
    return {
        'x_prompt': nrm(ks[0], (BATCH, SEQ, D_MODEL), 1.0),
        'x_sample': nrm(ks[1], (DEC_BATCH, DEC_SEQ, D_MODEL), 1.0),
        'cache_k': nrm(ks[2], (DEPTH, n_phys, PAGE_SIZE, ATT_KV_HEADS, ATT_HEAD_DIM), 1.0),
        'cache_v': nrm(ks[3], (DEPTH, n_phys, PAGE_SIZE, ATT_KV_HEADS, ATT_HEAD_DIM), 1.0),
        'page_table': page_table,
        'state_ssm': nrm(ks[5], (DEPTH, DEC_BATCH, SSD_HEADS, SSD_HEADDIM, SSD_STATE), 0.1),
        'state_ssm_conv': nrm(ks[6], (DEPTH, DEC_BATCH, SSD_CONV - 1, SSD_XBC), 1.0),
        'state_conv': nrm(ks[7], (DEPTH, DEC_BATCH, CONV_WIDTH - 1, CONV_CH), 0.5),
        'norm_mix': 1.0 + nrm(ks[8], (DEPTH, D_MODEL), 0.02),
        'w_in': nrm(ks[9], (DEPTH, D_MODEL, N_IN), D_MODEL ** -0.5),
        'ssd_conv_w': nrm(ks[11], (DEPTH, SSD_CONV, SSD_XBC), SSD_CONV ** -0.5),
        'ssd_conv_b': nrm(ks[12], (DEPTH, SSD_XBC), 0.02),
        'ssd_dt_bias': dt0 + jnp.log(-jnp.expm1(-dt0)),
        'ssd_a_log': jnp.log(jax.random.uniform(ks[13], (DEPTH, SSD_HEADS), jnp.float32, minval=1.0, maxval=16.0)),
        'ssd_d': 1.0 + nrm(ks[14], (DEPTH, SSD_HEADS), 0.1),
        'ssd_norm': 1.0 + nrm(ks[15], (DEPTH, D_MIX), 0.02),
        'w_ssd_out': nrm(ks[16], (DEPTH, D_MIX, D_MODEL), D_MIX ** -0.5),
        'conv_w': nrm(ks[17], (DEPTH, CONV_WIDTH, CONV_CH), CONV_WIDTH ** -0.5),
        'conv_b': nrm(ks[18], (DEPTH, CONV_CH), 0.02),
        'conv_ln_g': 1.0 + nrm(ks[19], (DEPTH, CONV_CH), 0.02),
        'conv_ln_b': nrm(ks[20], (DEPTH, CONV_CH), 0.02),
        'w_conv_out': nrm(ks[21], (DEPTH, CONV_CH, D_MODEL), CONV_CH ** -0.5),
        'w_att_out': nrm(ks[22], (DEPTH, ATT_HEADS * ATT_HEAD_DIM, D_MODEL), (ATT_HEADS * ATT_HEAD_DIM) ** -0.5),
        'w_o': nrm(ks[23], (DEPTH, D_MODEL, D_MODEL), D_MODEL ** -0.5),
        'norm_ffn': 1.0 + nrm(ks[24], (DEPTH, D_MODEL), 0.02),
        'w_up': nrm(ks[25], (DEPTH, D_MODEL, D_FF), D_MODEL ** -0.5),
        'w_down': nrm(ks[26], (DEPTH, D_FF, D_MODEL), D_FF ** -0.5),
        'norm_final': 1.0 + nrm(ks[27], (D_MODEL,), 0.02),
    }


def reference(x_prompt, x_sample, cache_k, cache_v, page_table, state_ssm, state_ssm_conv, state_conv,
              norm_mix, w_in, ssd_conv_w, ssd_conv_b, ssd_dt_bias, ssd_a_log, ssd_d, ssd_norm, w_ssd_out,
              conv_w, conv_b, conv_ln_g, conv_ln_b, w_conv_out, w_att_out, w_o, norm_ffn, w_up, w_down,
              norm_final):
    n_pages = PAST_LEN // PAGE_SIZE
    xp, xs = x_prompt, x_sample
    st_p, st_s = [], []
    for l in range(DEPTH):
        p = {'norm_mix': norm_mix[l], 'w_in': w_in[l], 'ssd_conv_w': ssd_conv_w[l], 'ssd_conv_b': ssd_conv_b[l],
             'ssd_dt_bias': ssd_dt_bias[l], 'ssd_a_log': ssd_a_log[l], 'ssd_d': ssd_d[l], 'ssd_norm': ssd_norm[l],
             'w_ssd_out': w_ssd_out[l], 'conv_w': conv_w[l], 'conv_b': conv_b[l], 'conv_ln_g': conv_ln_g[l],
             'conv_ln_b': conv_ln_b[l], 'w_conv_out': w_conv_out[l], 'w_att_out': w_att_out[l], 'w_o': w_o[l],
             'norm_ffn': norm_ffn[l], 'w_up': w_up[l], 'w_down': w_down[l]}
        bp = xp.shape[0]
        kv0 = jnp.zeros((bp, 0, ATT_KV_HEADS, ATT_HEAD_DIM), xp.dtype)
        xp, sp = _layer(xp, 0, kv0, kv0,
                        jnp.zeros((bp, SSD_HEADS, SSD_HEADDIM, SSD_STATE), jnp.float32),
                        jnp.zeros((bp, SSD_CONV - 1, SSD_XBC), xp.dtype),
                        jnp.zeros((bp, CONV_WIDTH - 1, CONV_CH), xp.dtype), p)
        k_past = cache_k[l][page_table].reshape(DEC_BATCH, n_pages * PAGE_SIZE, ATT_KV_HEADS, ATT_HEAD_DIM)
        v_past = cache_v[l][page_table].reshape(DEC_BATCH, n_pages * PAGE_SIZE, ATT_KV_HEADS, ATT_HEAD_DIM)
        xs, ss = _layer(xs, PAST_LEN, k_past, v_past, state_ssm[l], state_ssm_conv[l], state_conv[l], p)
        st_p.append(sp)
        st_s.append(ss)
    y_prompt = _rmsnorm(xp, norm_final)
    y_sample = _rmsnorm(xs, norm_final)
    k_prompt = jnp.stack([s[0] for s in st_p])
    v_prompt = jnp.stack([s[1] for s in st_p])
    k_sample = jnp.stack([s[0] for s in st_s])
    v_sample = jnp.stack([s[1] for s in st_s])
    ssm_prompt = jnp.stack([s[2] for s in st_p])
    ssm_sample = jnp.stack([s[2] for s in st_s])
    ssm_conv_prompt = jnp.stack([s[3] for s in st_p])
    ssm_conv_sample = jnp.stack([s[3] for s in st_s])
    conv_prompt = jnp.stack([s[4] for s in st_p])
    conv_sample = jnp.stack([s[4] for s in st_s])
    return (y_prompt, y_sample, k_prompt, v_prompt, k_sample, v_sample, ssm_prompt, ssm_sample,
            ssm_conv_prompt, ssm_conv_sample, conv_prompt, conv_sample)
```

```python
import functools
import math

import jax
import jax.numpy as jnp
from jax import lax
from jax.experimental import pallas as pl
from jax.experimental.pallas import tpu as pltpu

F32 = jnp.float32
BF16 = jnp.bfloat16

SSD_HEADDIM = 64
SSD_STATE = 128
SSD_GROUPS = 4
ATT_HEAD_DIM = 128
MOBA_BLOCK = 256
MOBA_TOPK = 3
ROPE_THETA = 10000.0
EPS = 1e-6

LANES = 128
SUBLANES = 8
VMEM_LIMIT = 56 * 1024 * 1024

NEG_INF = float("-inf")


def _pick_tile(n, cap, mult):
    best = None
    for d in range(mult, min(n, cap) + 1, mult):
        if n % d == 0:
            best = d
    return n if best is None else best


def _params(sem):
    return pltpu.CompilerParams(dimension_semantics=sem, vmem_limit_bytes=VMEM_LIMIT)


def _sigmoid(x):
    return jax.nn.sigmoid(x)


def _dot(a, b):
    return jnp.dot(a, b, preferred_element_type=F32)


def _dot_nt(a, b, precision=None):
    return lax.dot_general(a, b, (((1,), (1,)), ((), ())), preferred_element_type=F32,
                           precision=precision)


def _rmsnorm_kernel(x_ref, g_ref, o_ref):
    x = x_ref[...]
    ms = jnp.mean(x * x, axis=-1, keepdims=True)
    o_ref[...] = (x * lax.rsqrt(ms + EPS) * g_ref[...]).astype(o_ref.dtype)


def _rmsnorm(x, g, out_dtype):
    m, d = x.shape
    tr = _pick_tile(m, 256, 16)
    return pl.pallas_call(
        _rmsnorm_kernel,
        grid=(m // tr,),
        in_specs=[pl.BlockSpec((tr, d), lambda i: (i, 0)), pl.BlockSpec((1, d), lambda i: (0, 0))],
        out_specs=pl.BlockSpec((tr, d), lambda i: (i, 0)),
        out_shape=jax.ShapeDtypeStruct((m, d), out_dtype),
        compiler_params=_params(("parallel",)),
    )(x, g.reshape(1, d))


def _ep_store(acc, o_ref):
    o_ref[...] = acc.astype(o_ref.dtype)


def _ep_sigmoid(acc, o_ref):
    o_ref[...] = _sigmoid(acc).astype(o_ref.dtype)


def _ep_relu2(acc, o_ref):
    r = jnp.maximum(acc, 0.0)
    o_ref[...] = (r * r).astype(o_ref.dtype)


def _ep_glu(acc, o_ref):
    half = acc.shape[1] // 2
    o_ref[...] = (acc[:, :half] * _sigmoid(acc[:, half:])).astype(o_ref.dtype)


def _ep_residual(acc, o_ref, res_ref):
    o_ref[...] = (res_ref[...] + acc).astype(o_ref.dtype)


def _ep_rope(acc, o_ref, cos_ref, sin_ref, *, n_rope_tiles):
    j = pl.program_id(1)
    c = cos_ref[...]
    s = sin_ref[...]
    do_rope = j < n_rope_tiles
    for h in range(acc.shape[1] // ATT_HEAD_DIM):
        a = acc[:, h * ATT_HEAD_DIM:(h + 1) * ATT_HEAD_DIM]
        partner = pltpu.roll(a, ATT_HEAD_DIM // 2, 1)
        o_ref[:, h * ATT_HEAD_DIM:(h + 1) * ATT_HEAD_DIM] = jnp.where(do_rope, a * c + partner * s, a)


def _mm_kernel(*refs, nk, n_extra, epilogue):
    x_ref, w_ref = refs[0], refs[1]
    extra = refs[2:2 + n_extra]
    o_ref = refs[2 + n_extra]
    if nk == 1:
        epilogue(_dot(x_ref[...], w_ref[...]), o_ref, *extra)
        return
    acc_ref = refs[3 + n_extra]
    k = pl.program_id(2)

    @pl.when(k == 0)
    def _():
        acc_ref[...] = _dot(x_ref[...], w_ref[...])

    @pl.when(k > 0)
    def _():
        acc_ref[...] += _dot(x_ref[...], w_ref[...])

    @pl.when(k == nk - 1)
    def _():
        epilogue(acc_ref[...], o_ref, *extra)


def _matmul(x, w, *, epilogue=_ep_store, out_dtype=F32, tm_cap=1376, tn=512, tk=None,
            out_cols_per_tile=None, extra=(), extra_specs=()):
    m, kdim = x.shape
    n = w.shape[1]
    tm = _pick_tile(m, tm_cap, 16)
    tn = min(tn, n)
    tk = kdim if tk is None else tk
    assert n % tn == 0 and kdim % tk == 0
    nk = kdim // tk
    oc = tn if out_cols_per_tile is None else out_cols_per_tile
    grid = (m // tm, n // tn, nk)
    kern = functools.partial(_mm_kernel, nk=nk, n_extra=len(extra), epilogue=epilogue)
    scratch = [] if nk == 1 else [pltpu.VMEM((tm, tn), F32)]
    return pl.pallas_call(
        kern,
        grid=grid,
        in_specs=[pl.BlockSpec((tm, tk), lambda i, j, k: (i, k)),
                  pl.BlockSpec((tk, tn), lambda i, j, k: (k, j))] + [s(tm, tn) for s in extra_specs],
        out_specs=pl.BlockSpec((tm, oc), lambda i, j, k: (i, j)),
        out_shape=jax.ShapeDtypeStruct((m, (n // tn) * oc), out_dtype),
        scratch_shapes=scratch,
        compiler_params=_params(("parallel", "parallel", "arbitrary")),
    )(x, w, *extra)


def _spec_row_table(tm, tn):
    return pl.BlockSpec((tm, ATT_HEAD_DIM), lambda i, j, k: (i, 0))


def _spec_out_tile(tm, tn):
    return pl.BlockSpec((tm, tn), lambda i, j, k: (i, j))


def _merge_kernel(y_ref, w_ref, g_ref, o_ref, tot_ref, *, n_branch):
    b = pl.program_id(2)
    contrib = g_ref[...].astype(F32) * _dot(y_ref[...], w_ref[...])

    @pl.when(b == 0)
    def _():
        tot_ref[...] = contrib

    @pl.when(b > 0)
    def _():
        tot_ref[...] += contrib

    @pl.when(b == n_branch - 1)
    def _():
        o_ref[...] = tot_ref[...].astype(o_ref.dtype)


def _merge(y3, w3, g, tm_cap=1376, tn=1024):
    nb, m, kb = y3.shape
    d = w3.shape[2]
    tm = _pick_tile(m, tm_cap, 16)
    ncol = d // tn
    return pl.pallas_call(
        functools.partial(_merge_kernel, n_branch=nb),
        grid=(m // tm, ncol, nb),
        in_specs=[pl.BlockSpec((None, tm, kb), lambda i, j, b: (b, i, 0)),
                  pl.BlockSpec((None, kb, tn), lambda i, j, b: (b, 0, j)),
                  pl.BlockSpec((tm, tn), lambda i, j, b: (i, b * ncol + j))],
        out_specs=pl.BlockSpec((tm, tn), lambda i, j, b: (i, j)),
        out_shape=jax.ShapeDtypeStruct((m, d), BF16),
        scratch_shapes=[pltpu.VMEM((tm, tn), F32)],
        compiler_params=_params(("parallel", "parallel", "arbitrary")),
    )(y3, w3, g)


def _ssd_kernel(zx_ref, dt_ref, cw_ref, cb_ref, dtb_ref, alog_ref, dfull_ref, nw_ref, buf0_ref, h0_ref,
                y_ref, h_ref, ext_ref, act_ref, *, tc, t_valid, d_mix, kw):
    c = pl.program_id(1)
    n_state = SSD_STATE
    pair = 2 * SSD_HEADDIM
    grp_w = d_mix // SSD_GROUPS
    pairs_per_group = grp_w // pair

    @pl.when(c == 0)
    def _():
        h_ref[...] = h0_ref[...]
        ext_ref[0:SUBLANES, :] = buf0_ref[...]

    ext_ref[SUBLANES:SUBLANES + tc, :] = zx_ref[:, d_mix:]
    conv = cb_ref[...] + cw_ref[0:1, :] * ext_ref[SUBLANES - (kw - 1):SUBLANES - (kw - 1) + tc, :]
    for k in range(1, kw):
        off = SUBLANES - (kw - 1) + k
        conv = conv + cw_ref[k:k + 1, :] * ext_ref[off:off + tc, :]
    ext_ref[0:SUBLANES, :] = ext_ref[tc:tc + SUBLANES, :]
    act_ref[...] = conv * _sigmoid(conv)

    dt = jax.nn.softplus(dt_ref[...] + dtb_ref[...])
    row = lax.broadcasted_iota(jnp.int32, (tc, LANES), 0)
    if t_valid < tc:
        dt = jnp.where(row < t_valid, dt, 0.0)
    a = -jnp.exp(alog_ref[...])
    da = dt * a
    r_i = lax.broadcasted_iota(jnp.int32, (tc, tc), 0)
    c_i = lax.broadcasted_iota(jnp.int32, (tc, tc), 1)
    causal = c_i <= r_i
    tri = jnp.where(causal, 1.0, 0.0).astype(BF16)
    da_hi = da.astype(BF16)
    rem = da - da_hi.astype(F32)
    da_mid = rem.astype(BF16)
    da_lo = (rem - da_mid.astype(F32)).astype(BF16)
    acum = _dot(tri, da_hi) + _dot(tri, da_mid) + _dot(tri, da_lo)
    acum_t = acum.T
    dt_t = dt.T
    a_tot = acum[tc - 1:tc, :]
    to_end = jnp.exp(a_tot - acum) * dt
    eacum = jnp.exp(acum)
    cdec_t = jnp.exp(acum_t[:, tc - 1:tc])

    lane = lax.broadcasted_iota(jnp.int32, (tc, pair), 1)
    lo_lane = lane < SSD_HEADDIM
    prow = lax.broadcasted_iota(jnp.int32, (pair, n_state), 0)
    lo_row = prow < SSD_HEADDIM

    for g in range(SSD_GROUPS):
        b_g = act_ref[:, d_mix + g * n_state:d_mix + (g + 1) * n_state]
        c_g = act_ref[:, d_mix + (SSD_GROUPS + g) * n_state:d_mix + (SSD_GROUPS + g + 1) * n_state]
        c_bf = c_g.astype(BF16)
        cbm = _dot_nt(c_bf, b_g.astype(BF16))
        y_parts = []
        for pr in range(pairs_per_group):
            hp = g * pairs_per_group + pr
            xp = act_ref[:, hp * pair:(hp + 1) * pair]
            xp_bf = xp.astype(BF16)
            xp_t = xp.T.astype(BF16)
            hpair = h_ref[hp * pair:(hp + 1) * pair, :]
            ys, ss = [], []
            for h in (2 * hp, 2 * hp + 1):
                seg = acum[:, h:h + 1] - acum_t[h:h + 1, :]
                wm = jnp.where(causal, jnp.exp(seg), 0.0) * cbm * dt_t[h:h + 1, :]
                ys.append(_dot(wm.astype(BF16), xp_bf))
                bw = (b_g * to_end[:, h:h + 1]).astype(BF16)
                ss.append(_dot(xp_t, bw))
            y_pair = jnp.where(lo_lane, ys[0], ys[1])
            e_pair = jnp.where(lo_lane, eacum[:, 2 * hp:2 * hp + 1], eacum[:, 2 * hp + 1:2 * hp + 2])
            y_pair = y_pair + _dot_nt(c_bf, hpair.astype(BF16)) * e_pair
            s_pair = jnp.where(lo_row, ss[0], ss[1])
            cd_pair = jnp.where(lo_row, cdec_t[2 * hp:2 * hp + 1, :], cdec_t[2 * hp + 1:2 * hp + 2, :])
            h_ref[hp * pair:(hp + 1) * pair, :] = hpair * cd_pair + s_pair
            y_parts.append(y_pair)
        cols = slice(g * grp_w, (g + 1) * grp_w)
        yg = jnp.concatenate(y_parts, axis=1) + dfull_ref[:, cols] * act_ref[:, cols]
        z_g = zx_ref[:, cols]
        yg = yg * (z_g * _sigmoid(z_g))
        ms = jnp.mean(yg * yg, axis=-1, keepdims=True)
        y_ref[:, cols] = (yg * lax.rsqrt(ms + EPS) * nw_ref[:, cols]).astype(y_ref.dtype)


def _ssd(zx, dtr, p, buf0, h0, *, nb, nchunk, tc, t_valid, row_block0):
    d_mix = p["d_mix"]
    wz = zx.shape[1]
    xbc = wz - d_mix
    kw = p["ssd_conv_w"].shape[0]
    hp_rows = h0.shape[1]
    blk = lambda b, c: (row_block0 + b * nchunk + c, 0)
    const = lambda b, c: (0, 0)
    per_b = lambda b, c: (b, 0, 0)
    kern = functools.partial(_ssd_kernel, tc=tc, t_valid=t_valid, d_mix=d_mix, kw=kw)
    return pl.pallas_call(
        kern,
        grid=(nb, nchunk),
        in_specs=[pl.BlockSpec((tc, wz), blk), pl.BlockSpec((tc, LANES), blk),
                  pl.BlockSpec((kw, xbc), const), pl.BlockSpec((1, xbc), const),
                  pl.BlockSpec((1, LANES), const), pl.BlockSpec((1, LANES), const),
                  pl.BlockSpec((1, d_mix), const), pl.BlockSpec((1, d_mix), const),
                  pl.BlockSpec((None, SUBLANES, xbc), per_b), pl.BlockSpec((None, hp_rows, SSD_STATE), per_b)],
        out_specs=[pl.BlockSpec((tc, d_mix), lambda b, c: (b * nchunk + c, 0)),
                   pl.BlockSpec((None, hp_rows, SSD_STATE), per_b)],
        out_shape=[jax.ShapeDtypeStruct((nb * nchunk * tc, d_mix), BF16),
                   jax.ShapeDtypeStruct((nb, hp_rows, SSD_STATE), F32)],
        scratch_shapes=[pltpu.VMEM((tc + SUBLANES, xbc), F32), pltpu.VMEM((tc, xbc), F32)],
        compiler_params=_params(("parallel", "arbitrary")),
    )(zx, dtr, p["ssd_conv_w"], p["ssd_conv_b"], p["ssd_dt_bias"], p["ssd_a_log"], p["ssd_d_full"],
      p["ssd_norm"], buf0, h0)


CONV_HALO = 32


def _conv_kernel(u_ref, buf0_ref, w_ref, b_ref, g_ref, beta_ref, o_ref, ext_ref, acc_ref, *, tt, kw, rb, lb):
    i = pl.program_id(1)
    ch = u_ref.shape[1]

    @pl.when(i == 0)
    def _():
        ext_ref[0:CONV_HALO, :] = buf0_ref[...]

    ext_ref[CONV_HALO:CONV_HALO + tt, :] = u_ref[...]
    base = CONV_HALO - (kw - 1)

    def lane_block(cb, carry):
        c0 = pl.multiple_of(cb * lb, lb)
        for r in range(tt // rb):
            acc = jnp.broadcast_to(b_ref[:, pl.ds(c0, lb)], (rb, lb))
            for k in range(kw):
                acc = acc + ext_ref[pl.ds(r * rb + base + k, rb), pl.ds(c0, lb)] * w_ref[k:k + 1, pl.ds(c0, lb)]
            acc_ref[pl.ds(r * rb, rb), pl.ds(c0, lb)] = acc
        return carry

    lax.fori_loop(0, ch // lb, lane_block, 0)
    ext_ref[0:CONV_HALO, :] = ext_ref[tt:tt + CONV_HALO, :]

    x = acc_ref[...]
    mu = jnp.mean(x, axis=-1, keepdims=True)
    xc = x - mu
    var = jnp.mean(xc * xc, axis=-1, keepdims=True)
    y = xc * lax.rsqrt(var + EPS) * g_ref[...] + beta_ref[...]
    o_ref[...] = (y * _sigmoid(y)).astype(o_ref.dtype)


def _conv(u, p, buf0, *, nb, ntile, tt, row_block0):
    ch = u.shape[1]
    kw = p["conv_w"].shape[0]
    rb = min(tt, 32)
    lb = 256
    const = lambda b, i: (0, 0)
    kern = functools.partial(_conv_kernel, tt=tt, kw=kw, rb=rb, lb=lb)
    return pl.pallas_call(
        kern,
        grid=(nb, ntile),
        in_specs=[pl.BlockSpec((tt, ch), lambda b, i: (row_block0 + b * ntile + i, 0)),
                  pl.BlockSpec((None, CONV_HALO, ch), lambda b, i: (b, 0, 0)),
                  pl.BlockSpec((kw, ch), const), pl.BlockSpec((1, ch), const),
                  pl.BlockSpec((1, ch), const), pl.BlockSpec((1, ch), const)],
        out_specs=pl.BlockSpec((tt, ch), lambda b, i: (b * ntile + i, 0)),
        out_shape=jax.ShapeDtypeStruct((nb * ntile * tt, ch), BF16),
        scratch_shapes=[pltpu.VMEM((tt + CONV_HALO, ch), F32), pltpu.VMEM((tt, ch), F32)],
        compiler_params=_params(("parallel", "arbitrary")),
    )(u, buf0, p["conv_w"], p["conv_b"], p["conv_ln_g"], p["conv_ln_b"])


def _moba_prompt_kernel(q_ref, k_ref, v_ref, o_ref, kmean_ref, kb_ref, vb_ref, *, nblk, hpg, scale):
    i = pl.program_id(2)
    blk = MOBA_BLOCK
    hd = ATT_HEAD_DIM

    @pl.when(i == 0)
    def _():
        kf = k_ref[...]
        kb_ref[...] = kf.astype(BF16)
        vb_ref[...] = v_ref[...].astype(BF16)
        kmean_ref[...] = jnp.mean(kf.reshape(nblk, blk, hd), axis=1)

    km = kmean_ref[...]
    blk_ids = lax.broadcasted_iota(jnp.int32, (blk, nblk), 1)
    r_i = lax.broadcasted_iota(jnp.int32, (blk, blk), 0)
    c_i = lax.broadcasted_iota(jnp.int32, (blk, blk), 1)
    start = pl.multiple_of(i * blk, blk)
    kd = kb_ref[pl.ds(start, blk), :]
    vd = vb_ref[pl.ds(start, blk), :]

    for hh in range(hpg):
        qf = q_ref[:, hh * hd:(hh + 1) * hd]
        gate = _dot_nt(qf, km, precision=lax.Precision.HIGHEST)
        gate = jnp.where(blk_ids < i, gate, NEG_INF)
        rank = jnp.zeros((blk, nblk), jnp.int32)
        for jp in range(nblk):
            gj = gate[:, jp:jp + 1]
            beats = jnp.where(gj > gate, 1, jnp.where((gj == gate) & (blk_ids > jp), 1, 0))
            rank = rank + beats
        sel = jnp.where((rank < MOBA_TOPK) & (gate > NEG_INF), 1.0, 0.0)

        qb = qf.astype(BF16)
        s = _dot_nt(qb, kd) * scale
        s = jnp.where(c_i <= r_i, s, NEG_INF)
        m0 = jnp.max(s, axis=-1, keepdims=True)
        p0 = jnp.exp(s - m0)
        l0 = jnp.sum(p0, axis=-1, keepdims=True)
        acc0 = _dot(p0.astype(BF16), vd)

        def past_block(j, carry, qb=qb, sel=sel):
            m, l, acc = carry
            st = pl.multiple_of(j * blk, blk)
            sj = _dot_nt(qb, kb_ref[pl.ds(st, blk), :]) * scale
            sel_j = jnp.sum(jnp.where(blk_ids == j, sel, 0.0), axis=-1, keepdims=True)
            sj = jnp.where(sel_j > 0.0, sj, NEG_INF)
            m_new = jnp.maximum(m, jnp.max(sj, axis=-1, keepdims=True))
            alpha = jnp.exp(m - m_new)
            pj = jnp.exp(sj - m_new)
            l = alpha * l + jnp.sum(pj, axis=-1, keepdims=True)
            acc = alpha * acc + _dot(pj.astype(BF16), vb_ref[pl.ds(st, blk), :])
            return m_new, l, acc

        m, l, acc = lax.fori_loop(0, i, past_block, (m0, l0, acc0))
        o_ref[:, hh * hd:(hh + 1) * hd] = (acc / l).astype(o_ref.dtype)


def _moba_prompt(qkv, *, nb, t_len, n_heads, kv_heads):
    hd = ATT_HEAD_DIM
    blk = MOBA_BLOCK
    assert t_len % blk == 0
    nblk = t_len // blk
    hpg = n_heads // kv_heads
    kcol0 = n_heads
    vcol0 = n_heads + kv_heads
    kern = functools.partial(_moba_prompt_kernel, nblk=nblk, hpg=hpg, scale=hd ** -0.5)
    return pl.pallas_call(
        kern,
        grid=(nb, kv_heads, nblk),
        in_specs=[pl.BlockSpec((blk, hpg * hd), lambda b, g, i: (b * nblk + i, g)),
                  pl.BlockSpec((t_len, hd), lambda b, g, i: (b, kcol0 + g)),
                  pl.BlockSpec((t_len, hd), lambda b, g, i: (b, vcol0 + g))],
        out_specs=pl.BlockSpec((blk, hpg * hd), lambda b, g, i: (b * nblk + i, g)),
        out_shape=jax.ShapeDtypeStruct((nb * t_len, n_heads * hd), BF16),
        scratch_shapes=[pltpu.VMEM((nblk, hd), F32), pltpu.VMEM((t_len, hd), BF16),
                        pltpu.VMEM((t_len, hd), BF16)],
        compiler_params=_params(("parallel", "parallel", "arbitrary")),
    )(qkv, qkv, qkv)


def _moba_sample_kernel(pt_ref, q_ref, *refs, nblk, ppb, kv_heads, rows_per_group, dt_len, scale):
    del pt_ref
    kpages, vpages = refs[:ppb], refs[ppb:2 * ppb]
    knew_ref, vnew_ref, o_ref, gate_ref, m_ref, l_ref, acc_ref = refs[2 * ppb:]
    j = pl.program_id(1)
    hd = ATT_HEAD_DIM
    n_rows = q_ref.shape[0]
    q = q_ref[...]
    lane = lax.broadcasted_iota(jnp.int32, (n_rows, LANES), 1)

    @pl.when(j == 0)
    def _():
        gate_ref[...] = jnp.full((n_rows, LANES), NEG_INF, F32)
        m_ref[...] = jnp.zeros((n_rows, LANES), F32)
        l_ref[...] = jnp.zeros((n_rows, LANES), F32)

    kblk = jnp.concatenate([r[...] for r in kpages], axis=0)
    vblk = jnp.concatenate([r[...] for r in vpages], axis=0)

    def group_partials(g, kg, vg, mask=None):
        qg = q[g * rows_per_group:(g + 1) * rows_per_group, :]
        s = _dot_nt(qg.astype(BF16), kg.astype(BF16)) * scale
        if mask is not None:
            s = jnp.where(mask, s, NEG_INF)
        mg = jnp.max(s, axis=-1, keepdims=True)
        pg = jnp.exp(s - mg)
        return mg, jnp.sum(pg, axis=-1, keepdims=True), _dot(pg.astype(BF16), vg.astype(BF16))

    g_cols, m_cols, l_cols, accs = [], [], [], []
    for g in range(kv_heads):
        kg = kblk[:, g * hd:(g + 1) * hd]
        qg = q[g * rows_per_group:(g + 1) * rows_per_group, :]
        kmean = jnp.sum(kg, axis=0, keepdims=True) * (1.0 / MOBA_BLOCK)
        g_cols.append(jnp.sum(qg * kmean, axis=-1, keepdims=True))
        mg, lg, ag = group_partials(g, kg, vblk[:, g * hd:(g + 1) * hd])
        m_cols.append(mg)
        l_cols.append(lg)
        accs.append(ag)
    is_j = lane == j
    gate_ref[...] = jnp.where(is_j, jnp.concatenate(g_cols, axis=0), gate_ref[...])
    m_ref[...] = jnp.where(is_j, jnp.concatenate(m_cols, axis=0), m_ref[...])
    l_ref[...] = jnp.where(is_j, jnp.concatenate(l_cols, axis=0), l_ref[...])
    acc_ref[j] = jnp.concatenate(accs, axis=0)

    @pl.when(j == nblk - 1)
    def _():
        work = gate_ref[...]
        sel = jnp.zeros((n_rows, LANES), F32)
        for _ in range(min(MOBA_TOPK, nblk)):
            mx = jnp.max(work, axis=-1, keepdims=True)
            idx = jnp.min(jnp.where(work == mx, lane, LANES), axis=-1, keepdims=True)
            pick = lane == idx
            sel = jnp.where(pick & (mx > NEG_INF), 1.0, sel)
            work = jnp.where(pick, NEG_INF, work)
        row_t = lax.broadcasted_iota(jnp.int32, (rows_per_group, knew_ref.shape[0]), 0) % dt_len
        key_i = lax.broadcasted_iota(jnp.int32, (rows_per_group, knew_ref.shape[0]), 1)
        own = [group_partials(g, knew_ref[:, g * hd:(g + 1) * hd], vnew_ref[:, g * hd:(g + 1) * hd],
                              mask=key_i <= row_t) for g in range(kv_heads)]
        m_own = jnp.concatenate([o[0] for o in own], axis=0)
        l_own = jnp.concatenate([o[1] for o in own], axis=0)
        acc_own = jnp.concatenate([o[2] for o in own], axis=0)
        m_all = m_ref[...]
        picked = sel > 0.0
        m_tot = jnp.maximum(jnp.max(jnp.where(picked, m_all, NEG_INF), axis=-1, keepdims=True), m_own)
        w = jnp.where(picked, jnp.exp(m_all - m_tot), 0.0)
        w_own = jnp.exp(m_own - m_tot)
        l_tot = jnp.sum(w * l_ref[...], axis=-1, keepdims=True) + w_own * l_own
        out = w_own * acc_own
        for jj in range(nblk):
            out = out + w[:, jj:jj + 1] * acc_ref[jj]
        o_ref[...] = out / l_tot


def _moba_sample(q_rows, knew, vnew, cache_k, cache_v, page_table, layer, *, kv_heads, dt_len):
    db, n_rows, hd = q_rows.shape
    page = cache_k.shape[2]
    width = cache_k.shape[3]
    ppb = MOBA_BLOCK // page
    n_pages = page_table.shape[1]
    assert n_pages % ppb == 0 and MOBA_BLOCK % page == 0
    nblk = n_pages // ppb
    assert nblk <= LANES
    rows_per_group = n_rows // kv_heads

    def page_spec(o):
        return pl.BlockSpec((None, None, page, width), lambda b, j, pt: (layer, pt[b, ppb * j + o], 0, 0))

    per_b = lambda b, j, pt: (b, 0, 0)
    kern = functools.partial(_moba_sample_kernel, nblk=nblk, ppb=ppb, kv_heads=kv_heads,
                             rows_per_group=rows_per_group, dt_len=dt_len, scale=hd ** -0.5)
    grid_spec = pltpu.PrefetchScalarGridSpec(
        num_scalar_prefetch=1,
        grid=(db, nblk),
        in_specs=[pl.BlockSpec((None, n_rows, hd), per_b)]
                 + [page_spec(o) for o in range(ppb)] + [page_spec(o) for o in range(ppb)]
                 + [pl.BlockSpec((None, knew.shape[1], width), per_b)] * 2,
        out_specs=pl.BlockSpec((None, n_rows, hd), per_b),
        scratch_shapes=[pltpu.VMEM((n_rows, LANES), F32)] * 3 + [pltpu.VMEM((nblk, n_rows, hd), F32)],
    )
    return pl.pallas_call(
        kern,
        grid_spec=grid_spec,
        out_shape=jax.ShapeDtypeStruct((db, n_rows, hd), F32),
        compiler_params=_params(("parallel", "arbitrary")),
    )(page_table, q_rows, *([cache_k] * ppb), *([cache_v] * ppb), knew, vnew)


def _rope_tables(pos):
    half = ATT_HEAD_DIM // 2
    inv_freq = jnp.exp(-math.log(ROPE_THETA) * jnp.arange(half, dtype=F32) * 2.0 / ATT_HEAD_DIM)
    ang = pos[:, None] * inv_freq[None, :]
    cos, sin = jnp.cos(ang), jnp.sin(ang)
    return jnp.concatenate([cos, cos], axis=-1), jnp.concatenate([-sin, sin], axis=-1)


def _prep_layer(l, d_model, w_in, ssd_conv_w, ssd_conv_b, ssd_dt_bias, ssd_a_log, ssd_d, ssd_norm, w_ssd_out,
                conv_w, conv_b, conv_ln_g, conv_ln_b, w_conv_out, w_att_out, w_o, norm_mix, norm_ffn,
                w_up, w_down, n_heads, kv_heads, glu_tile):
    d_mix = d_model // 2
    ssd_heads = d_mix // SSD_HEADDIM
    xbc = d_mix + 2 * SSD_GROUPS * SSD_STATE
    conv_ch = conv_w.shape[2]
    o_zx, o_dt = 0, d_mix + xbc
    o_conv = o_dt + ssd_heads
    o_q = o_conv + 2 * conv_ch
    o_g = o_q + (n_heads + 2 * kv_heads) * ATT_HEAD_DIM
    wl = w_in[l]
    half = glu_tile // 2
    w_val = wl[:, o_conv:o_conv + conv_ch].reshape(d_model, conv_ch // half, half)
    w_gate = wl[:, o_conv + conv_ch:o_q].reshape(d_model, conv_ch // half, half)
    pad_row = lambda v: jnp.pad(v, (0, LANES - v.shape[0])).reshape(1, LANES)
    return {
        "d_mix": d_mix,
        "norm_mix": norm_mix[l], "norm_ffn": norm_ffn[l],
        "w_zx": wl[:, o_zx:o_dt].astype(BF16),
        "w_dt": jnp.pad(wl[:, o_dt:o_conv], ((0, 0), (0, LANES - ssd_heads))).astype(BF16),
        "w_glu": jnp.concatenate([w_val, w_gate], axis=2).reshape(d_model, 2 * conv_ch).astype(BF16),
        "w_qkv": wl[:, o_q:o_g].astype(BF16),
        "w_g": wl[:, o_g:].astype(BF16),
        "ssd_conv_w": ssd_conv_w[l], "ssd_conv_b": ssd_conv_b[l].reshape(1, xbc),
        "ssd_dt_bias": pad_row(ssd_dt_bias[l]), "ssd_a_log": pad_row(ssd_a_log[l]),
        "ssd_d_full": jnp.repeat(ssd_d[l], SSD_HEADDIM).reshape(1, d_mix),
        "ssd_norm": ssd_norm[l].reshape(1, d_mix),
        "conv_w": conv_w[l], "conv_b": conv_b[l].reshape(1, conv_ch),
        "conv_ln_g": conv_ln_g[l].reshape(1, conv_ch), "conv_ln_b": conv_ln_b[l].reshape(1, conv_ch),
        "w_branch": jnp.stack([w_ssd_out[l], w_conv_out[l], w_att_out[l]]).astype(BF16),
        "w_o": w_o[l].astype(BF16), "w_up": w_up[l].astype(BF16), "w_down": w_down[l].astype(BF16),
    }


def _layer(x, p, l, cos_t, sin_t, cache_k, cache_v, page_table, state_ssm_l, state_ssm_conv_l, state_conv_l,
           *, nbp, t_len, dbs, dt_len, n_heads, kv_heads, glu_tile):
    m, d_model = x.shape
    d_mix = p["d_mix"]
    hd = ATT_HEAD_DIM
    bt = nbp * t_len
    xbc_w = p["ssd_conv_w"].shape[1]
    conv_ch = p["conv_w"].shape[1]
    kv_w = kv_heads * hd

    h = _rmsnorm(x, p["norm_mix"], BF16)
    zx = _matmul(h, p["w_zx"])
    dtr = _matmul(h, p["w_dt"], tn=LANES)
    u = _matmul(h, p["w_glu"], epilogue=_ep_glu, tn=glu_tile, out_cols_per_tile=glu_tile // 2)
    n_rope_tiles = (n_heads + kv_heads) * hd // 512
    qkv = _matmul(h, p["w_qkv"], epilogue=functools.partial(_ep_rope, n_rope_tiles=n_rope_tiles),
                  extra=(cos_t, sin_t), extra_specs=(_spec_row_table, _spec_row_table))
    gates = _matmul(h, p["w_g"], epilogue=_ep_sigmoid, out_dtype=BF16)

    tc = 128
    ncp = t_len // tc
    y_p, ssm_p = _ssd(zx, dtr, p, jnp.zeros((nbp, SUBLANES, xbc_w), F32),
                      jnp.zeros((nbp, d_mix, SSD_STATE), F32),
                      nb=nbp, nchunk=ncp, tc=tc, t_valid=tc, row_block0=0)
    zx_s = zx[bt:].reshape(dbs, dt_len, -1)
    pad_t = ((0, 0), (0, tc - dt_len), (0, 0))
    zx_sp = jnp.pad(zx_s, pad_t).reshape(dbs * tc, -1)
    dtr_sp = jnp.pad(dtr[bt:].reshape(dbs, dt_len, LANES), pad_t).reshape(dbs * tc, LANES)
    kc = state_ssm_conv_l.shape[1]
    buf0_s = jnp.pad(state_ssm_conv_l, ((0, 0), (SUBLANES - kc, 0), (0, 0)))
    y_s, ssm_s = _ssd(zx_sp, dtr_sp, p, buf0_s, state_ssm_l.reshape(dbs, d_mix, SSD_STATE),
                      nb=dbs, nchunk=1, tc=tc, t_valid=dt_len, row_block0=0)
    y_s = y_s.reshape(dbs, tc, d_mix)[:, :dt_len].reshape(dbs * dt_len, d_mix)
    xbc_p = zx[:bt, d_mix:].reshape(nbp, t_len, xbc_w)
    ssm_buf_p = xbc_p[:, t_len - kc:]
    ssm_buf_s = jnp.concatenate([state_ssm_conv_l, zx_s[:, :, d_mix:]], axis=1)[:, -kc:]

    tt = 256
    cw1 = p["conv_w"].shape[0] - 1
    c_p = _conv(u, p, jnp.zeros((nbp, CONV_HALO, conv_ch), F32), nb=nbp, ntile=t_len // tt, tt=tt, row_block0=0)
    buf0_c = jnp.pad(state_conv_l, ((0, 0), (CONV_HALO - cw1, 0), (0, 0)))
    c_s = _conv(u, p, buf0_c, nb=dbs, ntile=1, tt=dt_len, row_block0=bt // dt_len)
    conv_buf_p = u[:bt].reshape(nbp, t_len, conv_ch)[:, t_len - cw1:]
    conv_buf_s = jnp.concatenate([state_conv_l, u[bt:].reshape(dbs, dt_len, conv_ch)], axis=1)[:, -cw1:]

    a_p = _moba_prompt(qkv, nb=nbp, t_len=t_len, n_heads=n_heads, kv_heads=kv_heads)
    qkv_s = qkv[bt:]
    q_rows = qkv_s[:, :n_heads * hd].reshape(dbs, dt_len, n_heads, hd).transpose(0, 2, 1, 3)
    q_rows = q_rows.reshape(dbs, n_heads * dt_len, hd)
    k_s = qkv_s[:, n_heads * hd:n_heads * hd + kv_w].reshape(dbs, dt_len, kv_w)
    v_s = qkv_s[:, n_heads * hd + kv_w:].reshape(dbs, dt_len, kv_w)
    pad_k = ((0, 0), (0, LANES - dt_len), (0, 0))
    a_s = _moba_sample(q_rows, jnp.pad(k_s, pad_k), jnp.pad(v_s, pad_k), cache_k, cache_v, page_table, l,
                       kv_heads=kv_heads, dt_len=dt_len)
    a_s = a_s.reshape(dbs, n_heads, dt_len, hd).transpose(0, 2, 1, 3).reshape(dbs * dt_len, n_heads * hd)
    k_p = qkv[:bt, n_heads * hd:n_heads * hd + kv_w].reshape(nbp, t_len, kv_heads, hd)
    v_p = qkv[:bt, n_heads * hd + kv_w:].reshape(nbp, t_len, kv_heads, hd)

    y3 = jnp.stack([jnp.concatenate([y_p, y_s]), jnp.concatenate([c_p, c_s]),
                    jnp.concatenate([a_p, a_s.astype(BF16)])])
    merged = _merge(y3, p["w_branch"], gates)
    res_spec = (_spec_out_tile,)
    x = _matmul(merged, p["w_o"], epilogue=_ep_residual, extra=(x,), extra_specs=res_spec)
    hf = _rmsnorm(x, p["norm_ffn"], BF16)
    act = _matmul(hf, p["w_up"], epilogue=_ep_relu2, out_dtype=BF16)
    x = _matmul(act, p["w_down"], epilogue=_ep_residual, extra=(x,), extra_specs=res_spec, tk=2048)
    states = (k_p, v_p, k_s.reshape(dbs, dt_len, kv_heads, hd), v_s.reshape(dbs, dt_len, kv_heads, hd),
              ssm_p.reshape(nbp, -1, SSD_HEADDIM, SSD_STATE), ssm_s.reshape(dbs, -1, SSD_HEADDIM, SSD_STATE),
              ssm_buf_p, ssm_buf_s, conv_buf_p, conv_buf_s)
    return x, states


def kernel(x_prompt, x_sample, cache_k, cache_v, page_table, state_ssm, state_ssm_conv, state_conv, norm_mix, w_in, ssd_conv_w, ssd_conv_b, ssd_dt_bias, ssd_a_log, ssd_d, ssd_norm, w_ssd_out, conv_w, conv_b, conv_ln_g, conv_ln_b, w_conv_out, w_att_out, w_o, norm_ffn, w_up, w_down, norm_final):
    nbp, t_len, d_model = x_prompt.shape
    dbs, dt_len, _ = x_sample.shape
    depth = w_in.shape[0]
    n_phys, page, kv_heads, hd = cache_k.shape[1:]
    assert hd == ATT_HEAD_DIM
    n_heads = w_att_out.shape[1] // hd
    past_len = page_table.shape[1] * page
    assert past_len % MOBA_BLOCK == 0 and dt_len <= SUBLANES
    bt = nbp * t_len
    glu_tile = 512

    x = jnp.concatenate([x_prompt.reshape(bt, d_model), x_sample.reshape(dbs * dt_len, d_model)], axis=0)
    pos = jnp.concatenate([jnp.tile(jnp.arange(t_len, dtype=F32), nbp),
                           jnp.tile(past_len + jnp.arange(dt_len, dtype=F32), dbs)])
    cos_t, sin_t = _rope_tables(pos)
    ck = cache_k.reshape(depth, n_phys, page, kv_heads * hd)
    cv = cache_v.reshape(depth, n_phys, page, kv_heads * hd)

    per_layer = []
    for l in range(depth):
        p = _prep_layer(l, d_model, w_in, ssd_conv_w, ssd_conv_b, ssd_dt_bias, ssd_a_log, ssd_d, ssd_norm,
                        w_ssd_out, conv_w, conv_b, conv_ln_g, conv_ln_b, w_conv_out, w_att_out, w_o,
                        norm_mix, norm_ffn, w_up, w_down, n_heads, kv_heads, glu_tile)
        x, st = _layer(x, p, l, cos_t, sin_t, ck, cv, page_table, state_ssm[l], state_ssm_conv[l],
                       state_conv[l], nbp=nbp, t_len=t_len, dbs=dbs, dt_len=dt_len, n_heads=n_heads,
                       kv_heads=kv_heads, glu_tile=glu_tile)
        per_layer.append(st)
    y = _rmsnorm(x, norm_final, F32)
    y_prompt = y[:bt].reshape(nbp, t_len, d_model)
    y_sample = y[bt:].reshape(dbs, dt_len, d_model)
    stacked = [jnp.stack([st[i] for st in per_layer]) for i in range(10)]
    return (y_prompt, y_sample, *stacked)
```

```python
import functools
import math

import jax
import jax.numpy as jnp
from jax import lax
from jax.experimental import pallas as pl
from jax.experimental.pallas import tpu as pltpu

F32 = jnp.float32
BF16 = jnp.bfloat16

SSD_HEADDIM = 64
SSD_STATE = 128
SSD_GROUPS = 4
ATT_HEAD_DIM = 128
MOBA_BLOCK = 256
MOBA_TOPK = 3
ROPE_THETA = 10000.0
EPS = 1e-6

LANES = 128
SUBLANES = 8
VMEM_LIMIT = 56 * 1024 * 1024

NEG_INF = float("-inf")


def _pick_tile(n, cap, mult):
    best = None
    for d in range(mult, min(n, cap) + 1, mult):
        if n % d == 0:
            best = d
    return n if best is None else best


def _params(sem):
    return pltpu.CompilerParams(dimension_semantics=sem, vmem_limit_bytes=VMEM_LIMIT)


def _sigmoid(x):
    return jax.nn.sigmoid(x)


def _dot(a, b):
    return jnp.dot(a, b, preferred_element_type=F32)


def _dot_nt(a, b, precision=None):
    return lax.dot_general(a, b, (((1,), (1,)), ((), ())), preferred_element_type=F32,
                           precision=precision)


def _rmsnorm_kernel(x_ref, g_ref, o_ref):
    x = x_ref[...]
    ms = jnp.mean(x * x, axis=-1, keepdims=True)
    o_ref[...] = (x * lax.rsqrt(ms + EPS) * g_ref[...]).astype(o_ref.dtype)


def _rmsnorm(x, g, out_dtype):
    m, d = x.shape
    tr = _pick_tile(m, 256, 16)
    return pl.pallas_call(
        _rmsnorm_kernel,
        grid=(m // tr,),
        in_specs=[pl.BlockSpec((tr, d), lambda i: (i, 0)), pl.BlockSpec((1, d), lambda i: (0, 0))],
        out_specs=pl.BlockSpec((tr, d), lambda i: (i, 0)),
        out_shape=jax.ShapeDtypeStruct((m, d), out_dtype),
        compiler_params=_params(("parallel",)),
    )(x, g.reshape(1, d))


def _ep_store(acc, o_ref):
    o_ref[...] = acc.astype(o_ref.dtype)


def _ep_sigmoid(acc, o_ref):
    o_ref[...] = _sigmoid(acc).astype(o_ref.dtype)


def _ep_relu2(acc, o_ref):
    r = jnp.maximum(acc, 0.0)
    o_ref[...] = (r * r).astype(o_ref.dtype)


def _ep_glu(acc, o_ref):
    half = acc.shape[1] // 2
    o_ref[...] = (acc[:, :half] * _sigmoid(acc[:, half:])).astype(o_ref.dtype)


def _ep_residual(acc, o_ref, res_ref):
    o_ref[...] = (res_ref[...] + acc).astype(o_ref.dtype)


def _ep_rope(acc, o_ref, cos_ref, sin_ref, *, n_rope_tiles):
    j = pl.program_id(1)
    c = cos_ref[...]
    s = sin_ref[...]
    do_rope = j < n_rope_tiles
    for h in range(acc.shape[1] // ATT_HEAD_DIM):
        a = acc[:, h * ATT_HEAD_DIM:(h + 1) * ATT_HEAD_DIM]
        partner = pltpu.roll(a, ATT_HEAD_DIM // 2, 1)
        o_ref[:, h * ATT_HEAD_DIM:(h + 1) * ATT_HEAD_DIM] = jnp.where(do_rope, a * c + partner * s, a)


def _mm_kernel(*refs, nk, n_extra, epilogue):
    x_ref, w_ref = refs[0], refs[1]
    extra = refs[2:2 + n_extra]
    o_ref = refs[2 + n_extra]
    if nk == 1:
        epilogue(_dot(x_ref[...], w_ref[...]), o_ref, *extra)
        return
    acc_ref = refs[3 + n_extra]
    k = pl.program_id(2)

    @pl.when(k == 0)
    def _():
        acc_ref[...] = _dot(x_ref[...], w_ref[...])

    @pl.when(k > 0)
    def _():
        acc_ref[...] += _dot(x_ref[...], w_ref[...])

    @pl.when(k == nk - 1)
    def _():
        epilogue(acc_ref[...], o_ref, *extra)


def _matmul(x, w, *, epilogue=_ep_store, out_dtype=F32, tm_cap=1376, tn=512, tk=None,
            out_cols_per_tile=None, extra=(), extra_specs=()):
    m, kdim = x.shape
    n = w.shape[1]
    tm = _pick_tile(m, tm_cap, 16)
    tn = min(tn, n)
    tk = kdim if tk is None else tk
    assert n % tn == 0 and kdim % tk == 0
    nk = kdim // tk
    oc = tn if out_cols_per_tile is None else out_cols_per_tile
    grid = (m // tm, n // tn, nk)
    kern = functools.partial(_mm_kernel, nk=nk, n_extra=len(extra), epilogue=epilogue)
    scratch = [] if nk == 1 else [pltpu.VMEM((tm, tn), F32)]
    return pl.pallas_call(
        kern,
        grid=grid,
        in_specs=[pl.BlockSpec((tm, tk), lambda i, j, k: (i, k)),
                  pl.BlockSpec((tk, tn), lambda i, j, k: (k, j))] + [s(tm, tn) for s in extra_specs],
        out_specs=pl.BlockSpec((tm, oc), lambda i, j, k: (i, j)),
        out_shape=jax.ShapeDtypeStruct((m, (n // tn) * oc), out_dtype),
        scratch_shapes=scratch,
        compiler_params=_params(("parallel", "parallel", "arbitrary")),
    )(x, w, *extra)


def _spec_row_table(tm, tn):
    return pl.BlockSpec((tm, ATT_HEAD_DIM), lambda i, j, k: (i, 0))


def _spec_out_tile(tm, tn):
    return pl.BlockSpec((tm, tn), lambda i, j, k: (i, j))


def _merge_kernel(y_ref, w_ref, g_ref, o_ref, tot_ref, *, n_branch):
    b = pl.program_id(2)
    contrib = g_ref[...].astype(F32) * _dot(y_ref[...], w_ref[...])

    @pl.when(b == 0)
    def _():
        tot_ref[...] = contrib

    @pl.when(b > 0)
    def _():
        tot_ref[...] += contrib

    @pl.when(b == n_branch - 1)
    def _():
        o_ref[...] = tot_ref[...].astype(o_ref.dtype)


def _merge(y3, w3, g, tm_cap=1376, tn=1024):
    nb, m, kb = y3.shape
    d = w3.shape[2]
    tm = _pick_tile(m, tm_cap, 16)
    ncol = d // tn
    return pl.pallas_call(
        functools.partial(_merge_kernel, n_branch=nb),
        grid=(m // tm, ncol, nb),
        in_specs=[pl.BlockSpec((None, tm, kb), lambda i, j, b: (b, i, 0)),
                  pl.BlockSpec((None, kb, tn), lambda i, j, b: (b, 0, j)),
                  pl.BlockSpec((tm, tn), lambda i, j, b: (i, b * ncol + j))],
        out_specs=pl.BlockSpec((tm, tn), lambda i, j, b: (i, j)),
        out_shape=jax.ShapeDtypeStruct((m, d), BF16),
        scratch_shapes=[pltpu.VMEM((tm, tn), F32)],
        compiler_params=_params(("parallel", "parallel", "arbitrary")),
    )(y3, w3, g)


def _ssd_kernel(zx_ref, dt_ref, cw_ref, cb_ref, dtb_ref, alog_ref, dfull_ref, nw_ref, buf0_ref, h0_ref,
                y_ref, h_ref, ext_ref, act_ref, *, tc, t_valid, d_mix, kw):
    c = pl.program_id(1)
    n_state = SSD_STATE
    pair = 2 * SSD_HEADDIM
    grp_w = d_mix // SSD_GROUPS
    pairs_per_group = grp_w // pair

    @pl.when(c == 0)
    def _():
        h_ref[...] = h0_ref[...]
        ext_ref[0:SUBLANES, :] = buf0_ref[...]

    ext_ref[SUBLANES:SUBLANES + tc, :] = zx_ref[:, d_mix:]
    conv = cb_ref[...] + cw_ref[0:1, :] * ext_ref[SUBLANES - (kw - 1):SUBLANES - (kw - 1) + tc, :]
    for k in range(1, kw):
        off = SUBLANES - (kw - 1) + k
        conv = conv + cw_ref[k:k + 1, :] * ext_ref[off:off + tc, :]
    ext_ref[0:SUBLANES, :] = ext_ref[tc:tc + SUBLANES, :]
    act_ref[...] = conv * _sigmoid(conv)

    dt = jax.nn.softplus(dt_ref[...] + dtb_ref[...])
    row = lax.broadcasted_iota(jnp.int32, (tc, LANES), 0)
    if t_valid < tc:
        dt = jnp.where(row < t_valid, dt, 0.0)
    a = -jnp.exp(alog_ref[...])
    da = dt * a
    r_i = lax.broadcasted_iota(jnp.int32, (tc, tc), 0)
    c_i = lax.broadcasted_iota(jnp.int32, (tc, tc), 1)
    causal = c_i <= r_i
    tri = jnp.where(causal, 1.0, 0.0).astype(BF16)
    da_hi = da.astype(BF16)
    rem = da - da_hi.astype(F32)
    da_mid = rem.astype(BF16)
    da_lo = (rem - da_mid.astype(F32)).astype(BF16)
    acum = _dot(tri, da_hi) + _dot(tri, da_mid) + _dot(tri, da_lo)
    acum_t = acum.T
    dt_t = dt.T
    a_tot = acum[tc - 1:tc, :]
    to_end = jnp.exp(a_tot - acum) * dt
    eacum = jnp.exp(acum)
    cdec_t = jnp.exp(acum_t[:, tc - 1:tc])

    lane = lax.broadcasted_iota(jnp.int32, (tc, pair), 1)
    lo_lane = lane < SSD_HEADDIM
    prow = lax.broadcasted_iota(jnp.int32, (pair, n_state), 0)
    lo_row = prow < SSD_HEADDIM

    for g in range(SSD_GROUPS):
        b_g = act_ref[:, d_mix + g * n_state:d_mix + (g + 1) * n_state]
        c_g = act_ref[:, d_mix + (SSD_GROUPS + g) * n_state:d_mix + (SSD_GROUPS + g + 1) * n_state]
        c_bf = c_g.astype(BF16)
        cbm = _dot_nt(c_bf, b_g.astype(BF16))
        y_parts = []
        for pr in range(pairs_per_group):
            hp = g * pairs_per_group + pr
            xp = act_ref[:, hp * pair:(hp + 1) * pair]
            xp_bf = xp.astype(BF16)
            xp_t = xp.T.astype(BF16)
            hpair = h_ref[hp * pair:(hp + 1) * pair, :]
            ys, ss = [], []
            for h in (2 * hp, 2 * hp + 1):
                seg = acum[:, h:h + 1] - acum_t[h:h + 1, :]
                wm = jnp.where(causal, jnp.exp(seg), 0.0) * cbm * dt_t[h:h + 1, :]
                ys.append(_dot(wm.astype(BF16), xp_bf))
                bw = (b_g * to_end[:, h:h + 1]).astype(BF16)
                ss.append(_dot(xp_t, bw))
            y_pair = jnp.where(lo_lane, ys[0], ys[1])
            e_pair = jnp.where(lo_lane, eacum[:, 2 * hp:2 * hp + 1], eacum[:, 2 * hp + 1:2 * hp + 2])
            y_pair = y_pair + _dot_nt(c_bf, hpair.astype(BF16)) * e_pair
            s_pair = jnp.where(lo_row, ss[0], ss[1])
            cd_pair = jnp.where(lo_row, cdec_t[2 * hp:2 * hp + 1, :], cdec_t[2 * hp + 1:2 * hp + 2, :])
            h_ref[hp * pair:(hp + 1) * pair, :] = hpair * cd_pair + s_pair
            y_parts.append(y_pair)
        cols = slice(g * grp_w, (g + 1) * grp_w)
        yg = jnp.concatenate(y_parts, axis=1) + dfull_ref[:, cols] * act_ref[:, cols]
        z_g = zx_ref[:, cols]
        yg = yg * (z_g * _sigmoid(z_g))
        ms = jnp.mean(yg * yg, axis=-1, keepdims=True)
        y_ref[:, cols] = (yg * lax.rsqrt(ms + EPS) * nw_ref[:, cols]).astype(y_ref.dtype)


def _ssd(zx, dtr, p, buf0, h0, *, nb, nchunk, tc, t_valid, row_block0):
    d_mix = p["d_mix"]
    wz = zx.shape[1]
    xbc = wz - d_mix
    kw = p["ssd_conv_w"].shape[0]
    hp_rows = h0.shape[1]
    blk = lambda b, c: (row_block0 + b * nchunk + c, 0)
    const = lambda b, c: (0, 0)
    per_b = lambda b, c: (b, 0, 0)
    kern = functools.partial(_ssd_kernel, tc=tc, t_valid=t_valid, d_mix=d_mix, kw=kw)
    return pl.pallas_call(
        kern,
        grid=(nb, nchunk),
        in_specs=[pl.BlockSpec((tc, wz), blk), pl.BlockSpec((tc, LANES), blk),
                  pl.BlockSpec((kw, xbc), const), pl.BlockSpec((1, xbc), const),
                  pl.BlockSpec((1, LANES), const), pl.BlockSpec((1, LANES), const),
                  pl.BlockSpec((1, d_mix), const), pl.BlockSpec((1, d_mix), const),
                  pl.BlockSpec((None, SUBLANES, xbc), per_b), pl.BlockSpec((None, hp_rows, SSD_STATE), per_b)],
        out_specs=[pl.BlockSpec((tc, d_mix), lambda b, c: (b * nchunk + c, 0)),
                   pl.BlockSpec((None, hp_rows, SSD_STATE), per_b)],
        out_shape=[jax.ShapeDtypeStruct((nb * nchunk * tc, d_mix), BF16),
                   jax.ShapeDtypeStruct((nb, hp_rows, SSD_STATE), F32)],
        scratch_shapes=[pltpu.VMEM((tc + SUBLANES, xbc), F32), pltpu.VMEM((tc, xbc), F32)],
        compiler_params=_params(("parallel", "arbitrary")),
    )(zx, dtr, p["ssd_conv_w"], p["ssd_conv_b"], p["ssd_dt_bias"], p["ssd_a_log"], p["ssd_d_full"],
      p["ssd_norm"], buf0, h0)


CONV_HALO = 32


def _conv_kernel(u_ref, buf0_ref, w_ref, b_ref, g_ref, beta_ref, o_ref, ext_ref, acc_ref, *, tt, kw, rb, lb):
    i = pl.program_id(1)
    ch = u_ref.shape[1]

    @pl.when(i == 0)
    def _():
        ext_ref[0:CONV_HALO, :] = buf0_ref[...]

    ext_ref[CONV_HALO:CONV_HALO + tt, :] = u_ref[...]
    base = CONV_HALO - (kw - 1)

    def lane_block(cb, carry):
        c0 = pl.multiple_of(cb * lb, lb)
        for r in range(tt // rb):
            acc = jnp.broadcast_to(b_ref[:, pl.ds(c0, lb)], (rb, lb))
            for k in range(kw):
                acc = acc + ext_ref[pl.ds(r * rb + base + k, rb), pl.ds(c0, lb)] * w_ref[k:k + 1, pl.ds(c0, lb)]
            acc_ref[pl.ds(r * rb, rb), pl.ds(c0, lb)] = acc
        return carry

    lax.fori_loop(0, ch // lb, lane_block, 0)
    ext_ref[0:CONV_HALO, :] = ext_ref[tt:tt + CONV_HALO, :]

    x = acc_ref[...]
    mu = jnp.mean(x, axis=-1, keepdims=True)
    xc = x - mu
    var = jnp.mean(xc * xc, axis=-1, keepdims=True)
    y = xc * lax.rsqrt(var + EPS) * g_ref[...] + beta_ref[...]
    o_ref[...] = (y * _sigmoid(y)).astype(o_ref.dtype)


def _conv(u, p, buf0, *, nb, ntile, tt, row_block0):
    ch = u.shape[1]
    kw = p["conv_w"].shape[0]
    rb = min(tt, 32)
    lb = 256
    const = lambda b, i: (0, 0)
    kern = functools.partial(_conv_kernel, tt=tt, kw=kw, rb=rb, lb=lb)
    return pl.pallas_call(
        kern,
        grid=(nb, ntile),
        in_specs=[pl.BlockSpec((tt, ch), lambda b, i: (row_block0 + b * ntile + i, 0)),
                  pl.BlockSpec((None, CONV_HALO, ch), lambda b, i: (b, 0, 0)),
                  pl.BlockSpec((kw, ch), const), pl.BlockSpec((1, ch), const),
                  pl.BlockSpec((1, ch), const), pl.BlockSpec((1, ch), const)],
        out_specs=pl.BlockSpec((tt, ch), lambda b, i: (b * ntile + i, 0)),
        out_shape=jax.ShapeDtypeStruct((nb * ntile * tt, ch), BF16),
        scratch_shapes=[pltpu.VMEM((tt + CONV_HALO, ch), F32), pltpu.VMEM((tt, ch), F32)],
        compiler_params=_params(("parallel", "arbitrary")),
    )(u, buf0, p["conv_w"], p["conv_b"], p["conv_ln_g"], p["conv_ln_b"])


def _moba_prompt_kernel(q_ref, k_ref, v_ref, o_ref, kmean_ref, kb_ref, vt_ref, qt_ref, sel_ref, acc_ref,
                        *, nblk, hpg, scale):
    i = pl.program_id(2)
    blk = MOBA_BLOCK
    hd = ATT_HEAD_DIM
    nq = hpg * blk
    nch = nq // LANES

    @pl.when(i == 0)
    def _():
        kmean_ref[...] = jnp.mean(k_ref[...].reshape(nblk, blk, hd), axis=1)
        for jb in range(nblk):
            kb_ref[jb] = k_ref[jb * blk:(jb + 1) * blk, :].astype(BF16)
            vt_ref[jb] = v_ref[jb * blk:(jb + 1) * blk, :].T.astype(BF16)

    qt = jnp.concatenate([q_ref[:, hh * hd:(hh + 1) * hd].T for hh in range(hpg)], axis=1)
    qt_ref[...] = qt.astype(BF16)
    gate = jnp.dot(kmean_ref[...], qt, preferred_element_type=F32, precision=lax.Precision.HIGHEST)
    blk_row = lax.broadcasted_iota(jnp.int32, (nblk, nq), 0)
    gate = jnp.where(blk_row < i, gate, NEG_INF)
    rank = jnp.zeros((nblk, nq), jnp.int32)
    for jp in range(nblk):
        gj = gate[jp:jp + 1, :]
        rank = rank + jnp.where(gj > gate, 1, jnp.where((gj == gate) & (blk_row > jp), 1, 0))
    sel = jnp.where((rank < MOBA_TOPK) & (gate > NEG_INF), 1.0, 0.0)
    for jb in range(nblk):
        sel_ref[jb] = jnp.broadcast_to(sel[jb:jb + 1, :], (SUBLANES, nq))

    key_i = lax.broadcasted_iota(jnp.int32, (blk, LANES), 0)
    lane_i = lax.broadcasted_iota(jnp.int32, (blk, LANES), 1)

    def block_update(j, m, l, diagonal):
        kj = kb_ref[j]
        vtj = vt_ref[j]
        m_out, l_out = [], []
        for c in range(nch):
            cs = slice(c * LANES, (c + 1) * LANES)
            s = _dot(kj, qt_ref[:, cs]) * scale
            if diagonal:
                q_off = (c * LANES) % blk
                s = jnp.where(key_i <= lane_i + q_off, s, NEG_INF)
                m_new = jnp.max(s, axis=0, keepdims=True)
                p = jnp.exp(s - m_new)
                l_out.append(jnp.sum(p, axis=0, keepdims=True))
                acc_ref[:, cs] = _dot(vtj, p.astype(BF16))
            else:
                s = jnp.where(sel_ref[j, 0:1, cs] > 0.0, s, NEG_INF)
                m_old = m[:, cs]
                m_new = jnp.maximum(m_old, jnp.max(s, axis=0, keepdims=True))
                alpha = jnp.exp(m_old - m_new)
                p = jnp.exp(s - m_new)
                l_out.append(alpha * l[:, cs] + jnp.sum(p, axis=0, keepdims=True))
                acc_ref[:, cs] = alpha * acc_ref[:, cs] + _dot(vtj, p.astype(BF16))
            m_out.append(m_new)
        return jnp.concatenate(m_out, axis=1), jnp.concatenate(l_out, axis=1)

    m0, l0 = block_update(i, None, None, True)
    m, l = lax.fori_loop(0, i, lambda j, ml: block_update(j, ml[0], ml[1], False), (m0, l0))
    out_t = acc_ref[...] / l
    for hh in range(hpg):
        o_ref[:, hh * hd:(hh + 1) * hd] = out_t[:, hh * blk:(hh + 1) * blk].T.astype(o_ref.dtype)


def _moba_prompt(qkv, *, nb, t_len, n_heads, kv_heads):
    hd = ATT_HEAD_DIM
    blk = MOBA_BLOCK
    assert t_len % blk == 0
    nblk = t_len // blk
    hpg = n_heads // kv_heads
    kcol0 = n_heads
    vcol0 = n_heads + kv_heads
    kern = functools.partial(_moba_prompt_kernel, nblk=nblk, hpg=hpg, scale=hd ** -0.5)
    return pl.pallas_call(
        kern,
        grid=(nb, kv_heads, nblk),
        in_specs=[pl.BlockSpec((blk, hpg * hd), lambda b, g, i: (b * nblk + i, g)),
                  pl.BlockSpec((t_len, hd), lambda b, g, i: (b, kcol0 + g)),
                  pl.BlockSpec((t_len, hd), lambda b, g, i: (b, vcol0 + g))],
        out_specs=pl.BlockSpec((blk, hpg * hd), lambda b, g, i: (b * nblk + i, g)),
        out_shape=jax.ShapeDtypeStruct((nb * t_len, n_heads * hd), BF16),
        scratch_shapes=[pltpu.VMEM((nblk, hd), F32), pltpu.VMEM((nblk, blk, hd), BF16),
                        pltpu.VMEM((nblk, hd, blk), BF16), pltpu.VMEM((hd, hpg * blk), BF16),
                        pltpu.VMEM((nblk, SUBLANES, hpg * blk), F32), pltpu.VMEM((hd, hpg * blk), F32)],
        compiler_params=_params(("parallel", "parallel", "arbitrary")),
    )(qkv, qkv, qkv)


def _moba_sample_kernel(pt_ref, q_ref, *refs, nblk, ppb, kv_heads, rows_per_group, dt_len, scale):
    del pt_ref
    kpages, vpages = refs[:ppb], refs[ppb:2 * ppb]
    knew_ref, vnew_ref, o_ref, gate_ref, m_ref, l_ref, acc_ref = refs[2 * ppb:]
    j = pl.program_id(1)
    hd = ATT_HEAD_DIM
    n_rows = q_ref.shape[0]
    rpg = rows_per_group
    n_keys = MOBA_BLOCK * kv_heads
    q = q_ref[...]
    lane = lax.broadcasted_iota(jnp.int32, (n_rows, LANES), 1)

    @pl.when(j == 0)
    def _():
        gate_ref[...] = jnp.full((n_rows, LANES), NEG_INF, F32)
        m_ref[...] = jnp.zeros((n_rows, LANES), F32)
        l_ref[...] = jnp.zeros((n_rows, LANES), F32)

    kblk = jnp.concatenate([r[...] for r in kpages], axis=0)
    vblk = jnp.concatenate([r[...] for r in vpages], axis=0)
    key_head = lax.broadcasted_iota(jnp.int32, (n_rows, n_keys), 1) % kv_heads
    row_head = lax.broadcasted_iota(jnp.int32, (n_rows, n_keys), 0) // rpg
    s = _dot_nt(q.astype(BF16), kblk.astype(BF16)) * scale
    s = jnp.where(key_head == row_head, s, NEG_INF)
    m_col = jnp.max(s, axis=-1, keepdims=True)
    p = jnp.exp(s - m_col)
    l_col = jnp.sum(p, axis=-1, keepdims=True)
    acc_ref[j] = _dot(p.astype(BF16), vblk.astype(BF16))
    ksub = jnp.sum(kblk.reshape(n_keys // SUBLANES, SUBLANES, hd), axis=0) * (1.0 / MOBA_BLOCK)
    gall = _dot_nt(q, ksub, precision=lax.Precision.HIGHEST)
    g_lane = lax.broadcasted_iota(jnp.int32, (n_rows, SUBLANES), 1) % kv_heads
    g_rowh = lax.broadcasted_iota(jnp.int32, (n_rows, SUBLANES), 0) // rpg
    g_col = jnp.sum(jnp.where(g_lane == g_rowh, gall, 0.0), axis=-1, keepdims=True)
    is_j = lane == j
    gate_ref[...] = jnp.where(is_j, g_col, gate_ref[...])
    m_ref[...] = jnp.where(is_j, m_col, m_ref[...])
    l_ref[...] = jnp.where(is_j, l_col, l_ref[...])

    def group_partials(g, kg, vg, mask):
        qg = q_ref[g * rpg:(g + 1) * rpg, :]
        sg = jnp.where(mask, _dot_nt(qg.astype(BF16), kg.astype(BF16)) * scale, NEG_INF)
        mg = jnp.max(sg, axis=-1, keepdims=True)
        pg = jnp.exp(sg - mg)
        return mg, jnp.sum(pg, axis=-1, keepdims=True), _dot(pg.astype(BF16), vg.astype(BF16))

    @pl.when(j == nblk - 1)
    def _():
        work = gate_ref[...]
        sel = jnp.zeros((n_rows, LANES), F32)
        for _ in range(min(MOBA_TOPK, nblk)):
            mx = jnp.max(work, axis=-1, keepdims=True)
            idx = jnp.min(jnp.where(work == mx, lane, LANES), axis=-1, keepdims=True)
            pick = lane == idx
            sel = jnp.where(pick & (mx > NEG_INF), 1.0, sel)
            work = jnp.where(pick, NEG_INF, work)
        row_t = lax.broadcasted_iota(jnp.int32, (rpg, knew_ref.shape[0]), 0) % dt_len
        key_i = lax.broadcasted_iota(jnp.int32, (rpg, knew_ref.shape[0]), 1)
        own = [group_partials(g, knew_ref[:, g * hd:(g + 1) * hd], vnew_ref[:, g * hd:(g + 1) * hd],
                              key_i <= row_t) for g in range(kv_heads)]
        m_own = jnp.concatenate([o[0] for o in own], axis=0)
        l_own = jnp.concatenate([o[1] for o in own], axis=0)
        acc_own = jnp.concatenate([o[2] for o in own], axis=0)
        m_all = m_ref[...]
        picked = sel > 0.0
        m_tot = jnp.maximum(jnp.max(jnp.where(picked, m_all, NEG_INF), axis=-1, keepdims=True), m_own)
        w = jnp.where(picked, jnp.exp(m_all - m_tot), 0.0)
        w_own = jnp.exp(m_own - m_tot)
        l_tot = jnp.sum(w * l_ref[...], axis=-1, keepdims=True) + w_own * l_own
        out = w_own * acc_own
        for jj in range(nblk):
            out = out + w[:, jj:jj + 1] * acc_ref[jj]
        o_ref[...] = out / l_tot


def _moba_sample(q_rows, knew, vnew, cache_k, cache_v, page_table, layer, *, kv_heads, dt_len):
    db, n_rows, hd = q_rows.shape
    page_rows = cache_k.shape[2]
    page = page_rows // kv_heads
    width = kv_heads * hd
    ppb = MOBA_BLOCK // page
    n_pages = page_table.shape[1]
    assert n_pages % ppb == 0 and MOBA_BLOCK % page == 0 and SUBLANES % kv_heads == 0
    nblk = n_pages // ppb
    assert nblk <= LANES
    rows_per_group = n_rows // kv_heads

    def page_spec(o):
        return pl.BlockSpec((None, None, page_rows, hd), lambda b, j, pt: (layer, pt[b, ppb * j + o], 0, 0))

    page_specs = [page_spec(o) for o in range(ppb)]
    per_b = lambda b, j, pt: (b, 0, 0)
    kern = functools.partial(_moba_sample_kernel, nblk=nblk, ppb=ppb, kv_heads=kv_heads,
                             rows_per_group=rows_per_group, dt_len=dt_len, scale=hd ** -0.5)
    grid_spec = pltpu.PrefetchScalarGridSpec(
        num_scalar_prefetch=1,
        grid=(db, nblk),
        in_specs=[pl.BlockSpec((None, n_rows, hd), per_b)]
                 + page_specs + page_specs
                 + [pl.BlockSpec((None, knew.shape[1], width), per_b)] * 2,
        out_specs=pl.BlockSpec((None, n_rows, hd), per_b),
        scratch_shapes=[pltpu.VMEM((n_rows, LANES), F32)] * 3 + [pltpu.VMEM((nblk, n_rows, hd), F32)],
    )
    return pl.pallas_call(
        kern,
        grid_spec=grid_spec,
        out_shape=jax.ShapeDtypeStruct((db, n_rows, hd), F32),
        compiler_params=_params(("parallel", "arbitrary")),
    )(page_table, q_rows, *([cache_k] * len(page_specs)), *([cache_v] * len(page_specs)), knew, vnew)


def _rope_tables(pos):
    half = ATT_HEAD_DIM // 2
    inv_freq = jnp.exp(-math.log(ROPE_THETA) * jnp.arange(half, dtype=F32) * 2.0 / ATT_HEAD_DIM)
    ang = pos[:, None] * inv_freq[None, :]
    cos, sin = jnp.cos(ang), jnp.sin(ang)
    return jnp.concatenate([cos, cos], axis=-1), jnp.concatenate([-sin, sin], axis=-1)


def _prep_layer(l, d_model, w_in, ssd_conv_w, ssd_conv_b, ssd_dt_bias, ssd_a_log, ssd_d, ssd_norm, w_ssd_out,
                conv_w, conv_b, conv_ln_g, conv_ln_b, w_conv_out, w_att_out, w_o, norm_mix, norm_ffn,
                w_up, w_down, n_heads, kv_heads, glu_tile):
    d_mix = d_model // 2
    ssd_heads = d_mix // SSD_HEADDIM
    xbc = d_mix + 2 * SSD_GROUPS * SSD_STATE
    conv_ch = conv_w.shape[2]
    o_zx, o_dt = 0, d_mix + xbc
    o_conv = o_dt + ssd_heads
    o_q = o_conv + 2 * conv_ch
    o_g = o_q + (n_heads + 2 * kv_heads) * ATT_HEAD_DIM
    wl = w_in[l]
    half = glu_tile // 2
    w_val = wl[:, o_conv:o_conv + conv_ch].reshape(d_model, conv_ch // half, half)
    w_gate = wl[:, o_conv + conv_ch:o_q].reshape(d_model, conv_ch // half, half)
    pad_row = lambda v: jnp.pad(v, (0, LANES - v.shape[0])).reshape(1, LANES)
    return {
        "d_mix": d_mix,
        "norm_mix": norm_mix[l], "norm_ffn": norm_ffn[l],
        "w_zx": wl[:, o_zx:o_dt].astype(BF16),
        "w_dt": jnp.pad(wl[:, o_dt:o_conv], ((0, 0), (0, LANES - ssd_heads))).astype(BF16),
        "w_glu": jnp.concatenate([w_val, w_gate], axis=2).reshape(d_model, 2 * conv_ch).astype(BF16),
        "w_qkv": wl[:, o_q:o_g].astype(BF16),
        "w_g": wl[:, o_g:].astype(BF16),
        "ssd_conv_w": ssd_conv_w[l], "ssd_conv_b": ssd_conv_b[l].reshape(1, xbc),
        "ssd_dt_bias": pad_row(ssd_dt_bias[l]), "ssd_a_log": pad_row(ssd_a_log[l]),
        "ssd_d_full": jnp.repeat(ssd_d[l], SSD_HEADDIM).reshape(1, d_mix),
        "ssd_norm": ssd_norm[l].reshape(1, d_mix),
        "conv_w": conv_w[l], "conv_b": conv_b[l].reshape(1, conv_ch),
        "conv_ln_g": conv_ln_g[l].reshape(1, conv_ch), "conv_ln_b": conv_ln_b[l].reshape(1, conv_ch),
        "w_branch": jnp.stack([w_ssd_out[l], w_conv_out[l], w_att_out[l]]).astype(BF16),
        "w_o": w_o[l].astype(BF16), "w_up": w_up[l].astype(BF16), "w_down": w_down[l].astype(BF16),
    }


def _layer(x, p, l, cos_t, sin_t, cache_k, cache_v, page_table, state_ssm_l, state_ssm_conv_l, state_conv_l,
           *, nbp, t_len, dbs, dt_len, n_heads, kv_heads, glu_tile):
    m, d_model = x.shape
    d_mix = p["d_mix"]
    hd = ATT_HEAD_DIM
    bt = nbp * t_len
    xbc_w = p["ssd_conv_w"].shape[1]
    conv_ch = p["conv_w"].shape[1]
    kv_w = kv_heads * hd

    h = _rmsnorm(x, p["norm_mix"], BF16)
    zx = _matmul(h, p["w_zx"])
    dtr = _matmul(h, p["w_dt"], tn=LANES)
    u = _matmul(h, p["w_glu"], epilogue=_ep_glu, tn=glu_tile, out_cols_per_tile=glu_tile // 2)
    n_rope_tiles = (n_heads + kv_heads) * hd // 512
    qkv = _matmul(h, p["w_qkv"], epilogue=functools.partial(_ep_rope, n_rope_tiles=n_rope_tiles),
                  extra=(cos_t, sin_t), extra_specs=(_spec_row_table, _spec_row_table))
    gates = _matmul(h, p["w_g"], epilogue=_ep_sigmoid, out_dtype=BF16)

    tc = 128
    ncp = t_len // tc
    y_p, ssm_p = _ssd(zx, dtr, p, jnp.zeros((nbp, SUBLANES, xbc_w), F32),
                      jnp.zeros((nbp, d_mix, SSD_STATE), F32),
                      nb=nbp, nchunk=ncp, tc=tc, t_valid=tc, row_block0=0)
    zx_s = zx[bt:].reshape(dbs, dt_len, -1)
    pad_t = ((0, 0), (0, tc - dt_len), (0, 0))
    zx_sp = jnp.pad(zx_s, pad_t).reshape(dbs * tc, -1)
    dtr_sp = jnp.pad(dtr[bt:].reshape(dbs, dt_len, LANES), pad_t).reshape(dbs * tc, LANES)
    kc = state_ssm_conv_l.shape[1]
    buf0_s = jnp.pad(state_ssm_conv_l, ((0, 0), (SUBLANES - kc, 0), (0, 0)))
    y_s, ssm_s = _ssd(zx_sp, dtr_sp, p, buf0_s, state_ssm_l.reshape(dbs, d_mix, SSD_STATE),
                      nb=dbs, nchunk=1, tc=tc, t_valid=dt_len, row_block0=0)
    y_s = y_s.reshape(dbs, tc, d_mix)[:, :dt_len].reshape(dbs * dt_len, d_mix)
    xbc_p = zx[:bt, d_mix:].reshape(nbp, t_len, xbc_w)
    ssm_buf_p = xbc_p[:, t_len - kc:]
    ssm_buf_s = jnp.concatenate([state_ssm_conv_l, zx_s[:, :, d_mix:]], axis=1)[:, -kc:]

    tt = 256
    cw1 = p["conv_w"].shape[0] - 1
    c_p = _conv(u, p, jnp.zeros((nbp, CONV_HALO, conv_ch), F32), nb=nbp, ntile=t_len // tt, tt=tt, row_block0=0)
    buf0_c = jnp.pad(state_conv_l, ((0, 0), (CONV_HALO - cw1, 0), (0, 0)))
    c_s = _conv(u, p, buf0_c, nb=dbs, ntile=1, tt=dt_len, row_block0=bt // dt_len)
    conv_buf_p = u[:bt].reshape(nbp, t_len, conv_ch)[:, t_len - cw1:]
    conv_buf_s = jnp.concatenate([state_conv_l, u[bt:].reshape(dbs, dt_len, conv_ch)], axis=1)[:, -cw1:]

    a_p = _moba_prompt(qkv, nb=nbp, t_len=t_len, n_heads=n_heads, kv_heads=kv_heads)
    qkv_s = qkv[bt:]
    q_rows = qkv_s[:, :n_heads * hd].reshape(dbs, dt_len, n_heads, hd).transpose(0, 2, 1, 3)
    q_rows = q_rows.reshape(dbs, n_heads * dt_len, hd)
    k_s = qkv_s[:, n_heads * hd:n_heads * hd + kv_w].reshape(dbs, dt_len, kv_w)
    v_s = qkv_s[:, n_heads * hd + kv_w:].reshape(dbs, dt_len, kv_w)
    pad_k = ((0, 0), (0, LANES - dt_len), (0, 0))
    a_s = _moba_sample(q_rows, jnp.pad(k_s, pad_k), jnp.pad(v_s, pad_k), cache_k, cache_v, page_table, l,
                       kv_heads=kv_heads, dt_len=dt_len)
    a_s = a_s.reshape(dbs, n_heads, dt_len, hd).transpose(0, 2, 1, 3).reshape(dbs * dt_len, n_heads * hd)
    k_p = qkv[:bt, n_heads * hd:n_heads * hd + kv_w].reshape(nbp, t_len, kv_heads, hd)
    v_p = qkv[:bt, n_heads * hd + kv_w:].reshape(nbp, t_len, kv_heads, hd)

    y3 = jnp.stack([jnp.concatenate([y_p, y_s]), jnp.concatenate([c_p, c_s]),
                    jnp.concatenate([a_p, a_s.astype(BF16)])])
    merged = _merge(y3, p["w_branch"], gates)
    res_spec = (_spec_out_tile,)
    x = _matmul(merged, p["w_o"], epilogue=_ep_residual, extra=(x,), extra_specs=res_spec)
    hf = _rmsnorm(x, p["norm_ffn"], BF16)
    act = _matmul(hf, p["w_up"], epilogue=_ep_relu2, out_dtype=BF16)
    x = _matmul(act, p["w_down"], epilogue=_ep_residual, extra=(x,), extra_specs=res_spec, tk=2048)
    states = (k_p, v_p, k_s.reshape(dbs, dt_len, kv_heads, hd), v_s.reshape(dbs, dt_len, kv_heads, hd),
              ssm_p.reshape(nbp, -1, SSD_HEADDIM, SSD_STATE), ssm_s.reshape(dbs, -1, SSD_HEADDIM, SSD_STATE),
              ssm_buf_p, ssm_buf_s, conv_buf_p, conv_buf_s)
    return x, states


def kernel(x_prompt, x_sample, cache_k, cache_v, page_table, state_ssm, state_ssm_conv, state_conv, norm_mix, w_in, ssd_conv_w, ssd_conv_b, ssd_dt_bias, ssd_a_log, ssd_d, ssd_norm, w_ssd_out, conv_w, conv_b, conv_ln_g, conv_ln_b, w_conv_out, w_att_out, w_o, norm_ffn, w_up, w_down, norm_final):
    nbp, t_len, d_model = x_prompt.shape
    dbs, dt_len, _ = x_sample.shape
    depth = w_in.shape[0]
    n_phys, page, kv_heads, hd = cache_k.shape[1:]
    assert hd == ATT_HEAD_DIM
    n_heads = w_att_out.shape[1] // hd
    past_len = page_table.shape[1] * page
    assert past_len % MOBA_BLOCK == 0 and dt_len <= SUBLANES
    bt = nbp * t_len
    glu_tile = 512

    x = jnp.concatenate([x_prompt.reshape(bt, d_model), x_sample.reshape(dbs * dt_len, d_model)], axis=0)
    pos = jnp.concatenate([jnp.tile(jnp.arange(t_len, dtype=F32), nbp),
                           jnp.tile(past_len + jnp.arange(dt_len, dtype=F32), dbs)])
    cos_t, sin_t = _rope_tables(pos)
    ck = cache_k.reshape(depth, n_phys, page * kv_heads, hd)
    cv = cache_v.reshape(depth, n_phys, page * kv_heads, hd)

    per_layer = []
    for l in range(depth):
        p = _prep_layer(l, d_model, w_in, ssd_conv_w, ssd_conv_b, ssd_dt_bias, ssd_a_log, ssd_d, ssd_norm,
                        w_ssd_out, conv_w, conv_b, conv_ln_g, conv_ln_b, w_conv_out, w_att_out, w_o,
                        norm_mix, norm_ffn, w_up, w_down, n_heads, kv_heads, glu_tile)
        x, st = _layer(x, p, l, cos_t, sin_t, ck, cv, page_table, state_ssm[l], state_ssm_conv[l],
                       state_conv[l], nbp=nbp, t_len=t_len, dbs=dbs, dt_len=dt_len, n_heads=n_heads,
                       kv_heads=kv_heads, glu_tile=glu_tile)
        per_layer.append(st)
    y = _rmsnorm(x, norm_final, F32)
    y_prompt = y[:bt].reshape(nbp, t_len, d_model)
    y_sample = y[bt:].reshape(dbs, dt_len, d_model)
    stacked = [jnp.stack([st[i] for st in per_layer]) for i in range(10)]
    return (y_prompt, y_sample, *stacked)
```

```python
import functools
import math

import jax
import jax.numpy as jnp
from jax import lax
from jax.experimental import pallas as pl
from jax.experimental.pallas import tpu as pltpu

F32 = jnp.float32
BF16 = jnp.bfloat16

SSD_HEADDIM = 64
SSD_STATE = 128
SSD_GROUPS = 4
ATT_HEAD_DIM = 128
MOBA_BLOCK = 256
MOBA_TOPK = 3
ROPE_THETA = 10000.0
EPS = 1e-6

LANES = 128
SUBLANES = 8
VMEM_LIMIT = 56 * 1024 * 1024

NEG_INF = float("-inf")


def _pick_tile(n, cap, mult):
    best = None
    for d in range(mult, min(n, cap) + 1, mult):
        if n % d == 0:
            best = d
    return n if best is None else best


def _params(sem):
    return pltpu.CompilerParams(dimension_semantics=sem, vmem_limit_bytes=VMEM_LIMIT)


def _sigmoid(x):
    return jax.nn.sigmoid(x)


def _dot(a, b):
    return jnp.dot(a, b, preferred_element_type=F32)


def _dot_nt(a, b, precision=None):
    return lax.dot_general(a, b, (((1,), (1,)), ((), ())), preferred_element_type=F32,
                           precision=precision)


def _rmsnorm_kernel(x_ref, g_ref, o_ref):
    x = x_ref[...]
    ms = jnp.mean(x * x, axis=-1, keepdims=True)
    o_ref[...] = (x * lax.rsqrt(ms + EPS) * g_ref[...]).astype(o_ref.dtype)


def _rmsnorm(x, g, out_dtype, row0=0, rows=None):
    d = x.shape[1]
    rows = x.shape[0] if rows is None else rows
    tr = _pick_tile(math.gcd(rows, row0) if row0 else rows, 256, 16)
    blk0 = row0 // tr
    return pl.pallas_call(
        _rmsnorm_kernel,
        grid=(rows // tr,),
        in_specs=[pl.BlockSpec((tr, d), lambda i: (blk0 + i, 0)), pl.BlockSpec((1, d), lambda i: (0, 0))],
        out_specs=pl.BlockSpec((tr, d), lambda i: (i, 0)),
        out_shape=jax.ShapeDtypeStruct((rows, d), out_dtype),
        compiler_params=_params(("parallel",)),
    )(x, g.reshape(1, d))


def _ep_store(acc, o_ref):
    o_ref[...] = acc.astype(o_ref.dtype)


def _ep_sigmoid(acc, o_ref):
    o_ref[...] = _sigmoid(acc).astype(o_ref.dtype)


def _ep_relu2(acc, o_ref):
    r = jnp.maximum(acc, 0.0)
    o_ref[...] = (r * r).astype(o_ref.dtype)


def _ep_residual(acc, o_ref, res_ref):
    o_ref[...] = (res_ref[...] + acc).astype(o_ref.dtype)


def _ep_rope(acc, o_ref, cos_ref, sin_ref, *, n_rope_tiles):
    j = pl.program_id(1)
    c = cos_ref[...]
    s = sin_ref[...]
    do_rope = j < n_rope_tiles
    for h in range(acc.shape[1] // ATT_HEAD_DIM):
        a = acc[:, h * ATT_HEAD_DIM:(h + 1) * ATT_HEAD_DIM]
        partner = pltpu.roll(a, ATT_HEAD_DIM // 2, 1)
        o_ref[:, h * ATT_HEAD_DIM:(h + 1) * ATT_HEAD_DIM] = jnp.where(do_rope, a * c + partner * s, a)


def _mm_kernel(*refs, nk, n_extra, epilogue):
    x_ref, w_ref = refs[0], refs[1]
    extra = refs[2:2 + n_extra]
    o_ref = refs[2 + n_extra]
    if nk == 1:
        epilogue(_dot(x_ref[...], w_ref[...]), o_ref, *extra)
        return
    acc_ref = refs[3 + n_extra]
    k = pl.program_id(2)

    @pl.when(k == 0)
    def _():
        acc_ref[...] = _dot(x_ref[...], w_ref[...])

    @pl.when(k > 0)
    def _():
        acc_ref[...] += _dot(x_ref[...], w_ref[...])

    @pl.when(k == nk - 1)
    def _():
        epilogue(acc_ref[...], o_ref, *extra)


def _matmul(x, w, layer, *, epilogue=_ep_store, out_dtype=F32, tm_cap=1376, tn=512, tk=None,
            extra=(), extra_specs=()):
    m, kdim = x.shape
    n = w.shape[2]
    tm = _pick_tile(m, tm_cap, 16)
    tn = min(tn, n)
    tk = kdim if tk is None else tk
    assert n % tn == 0 and kdim % tk == 0
    nk = kdim // tk
    grid = (m // tm, n // tn, nk)
    kern = functools.partial(_mm_kernel, nk=nk, n_extra=len(extra), epilogue=epilogue)
    scratch = [] if nk == 1 else [pltpu.VMEM((tm, tn), F32)]
    return pl.pallas_call(
        kern,
        grid=grid,
        in_specs=[pl.BlockSpec((tm, tk), lambda i, j, k: (i, k)),
                  pl.BlockSpec((None, tk, tn), lambda i, j, k: (layer, k, j))] + [s(tm, tn) for s in extra_specs],
        out_specs=pl.BlockSpec((tm, tn), lambda i, j, k: (i, j)),
        out_shape=jax.ShapeDtypeStruct((m, n), out_dtype),
        scratch_shapes=scratch,
        compiler_params=_params(("parallel", "parallel", "arbitrary")),
    )(x, w, *extra)


def _glu_kernel(x_ref, wv_ref, wg_ref, o_ref):
    x = x_ref[...]
    o_ref[...] = (_dot(x, wv_ref[...]) * _sigmoid(_dot(x, wg_ref[...]))).astype(o_ref.dtype)


def _matmul_glu(x, w_val, w_gate, layer, *, tm_cap=1376, tn=256):
    m, kdim = x.shape
    n = w_val.shape[2]
    tm = _pick_tile(m, tm_cap, 16)
    assert n % tn == 0
    wspec = pl.BlockSpec((None, kdim, tn), lambda i, j: (layer, 0, j))
    return pl.pallas_call(
        _glu_kernel,
        grid=(m // tm, n // tn),
        in_specs=[pl.BlockSpec((tm, kdim), lambda i, j: (i, 0)), wspec, wspec],
        out_specs=pl.BlockSpec((tm, tn), lambda i, j: (i, j)),
        out_shape=jax.ShapeDtypeStruct((m, n), F32),
        compiler_params=_params(("parallel", "parallel")),
    )(x, w_val, w_gate)


def _spec_row_table(tm, tn):
    return pl.BlockSpec((tm, ATT_HEAD_DIM), lambda i, j, k: (i, 0))


def _spec_out_tile(tm, tn):
    return pl.BlockSpec((tm, tn), lambda i, j, k: (i, j))


def _merge_kernel(y_ref, w_ref, g_ref, o_ref, tot_ref, *, n_branch):
    b = pl.program_id(2)
    contrib = g_ref[...].astype(F32) * _dot(y_ref[...], w_ref[...])

    @pl.when(b == 0)
    def _():
        tot_ref[...] = contrib

    @pl.when(b > 0)
    def _():
        tot_ref[...] += contrib

    @pl.when(b == n_branch - 1)
    def _():
        o_ref[...] = tot_ref[...].astype(o_ref.dtype)


def _merge(y3, w3, g, layer, tm_cap=1376, tn=1024):
    nb, m, kb = y3.shape
    d = w3.shape[3]
    tm = _pick_tile(m, tm_cap, 16)
    ncol = d // tn
    return pl.pallas_call(
        functools.partial(_merge_kernel, n_branch=nb),
        grid=(m // tm, ncol, nb),
        in_specs=[pl.BlockSpec((None, tm, kb), lambda i, j, b: (b, i, 0)),
                  pl.BlockSpec((None, None, kb, tn), lambda i, j, b: (b, layer, 0, j)),
                  pl.BlockSpec((tm, tn), lambda i, j, b: (i, b * ncol + j))],
        out_specs=pl.BlockSpec((tm, tn), lambda i, j, b: (i, j)),
        out_shape=jax.ShapeDtypeStruct((m, d), BF16),
        scratch_shapes=[pltpu.VMEM((tm, tn), F32)],
        compiler_params=_params(("parallel", "parallel", "arbitrary")),
    )(y3, w3, g)


def _ssd_kernel(zx_ref, dt_ref, cw_ref, cb_ref, dtb_ref, alog_ref, dfull_ref, nw_ref, buf0_ref, h0_ref,
                y_ref, h_ref, ext_ref, act_ref, *, tc, t_valid, d_mix, kw):
    c = pl.program_id(1)
    n_state = SSD_STATE
    pair = 2 * SSD_HEADDIM
    grp_w = d_mix // SSD_GROUPS
    pairs_per_group = grp_w // pair

    @pl.when(c == 0)
    def _():
        h_ref[...] = h0_ref[...]
        ext_ref[0:SUBLANES, :] = buf0_ref[...]

    ext_ref[SUBLANES:SUBLANES + tc, :] = zx_ref[:, d_mix:]
    conv = cb_ref[...] + cw_ref[0:1, :] * ext_ref[SUBLANES - (kw - 1):SUBLANES - (kw - 1) + tc, :]
    for k in range(1, kw):
        off = SUBLANES - (kw - 1) + k
        conv = conv + cw_ref[k:k + 1, :] * ext_ref[off:off + tc, :]
    ext_ref[0:SUBLANES, :] = ext_ref[tc:tc + SUBLANES, :]
    act_ref[...] = conv * _sigmoid(conv)

    dt = jax.nn.softplus(dt_ref[...] + dtb_ref[...])
    row = lax.broadcasted_iota(jnp.int32, (tc, LANES), 0)
    if t_valid < tc:
        dt = jnp.where(row < t_valid, dt, 0.0)
    a = -jnp.exp(alog_ref[...])
    da = dt * a
    r_i = lax.broadcasted_iota(jnp.int32, (tc, tc), 0)
    c_i = lax.broadcasted_iota(jnp.int32, (tc, tc), 1)
    causal = c_i <= r_i
    tri = jnp.where(causal, 1.0, 0.0).astype(BF16)
    da_hi = da.astype(BF16)
    rem = da - da_hi.astype(F32)
    da_mid = rem.astype(BF16)
    da_lo = (rem - da_mid.astype(F32)).astype(BF16)
    acum = _dot(tri, da_hi) + _dot(tri, da_mid) + _dot(tri, da_lo)
    acum_t = acum.T
    dt_t = dt.T
    a_tot = acum[tc - 1:tc, :]
    to_end = jnp.exp(a_tot - acum) * dt
    eacum = jnp.exp(acum)
    cdec_t = jnp.exp(acum_t[:, tc - 1:tc])

    lane = lax.broadcasted_iota(jnp.int32, (tc, pair), 1)
    lo_lane = lane < SSD_HEADDIM
    prow = lax.broadcasted_iota(jnp.int32, (pair, n_state), 0)
    lo_row = prow < SSD_HEADDIM

    for g in range(SSD_GROUPS):
        b_g = act_ref[:, d_mix + g * n_state:d_mix + (g + 1) * n_state]
        c_g = act_ref[:, d_mix + (SSD_GROUPS + g) * n_state:d_mix + (SSD_GROUPS + g + 1) * n_state]
        c_bf = c_g.astype(BF16)
        cbm = _dot_nt(c_bf, b_g.astype(BF16))
        y_parts = []
        for pr in range(pairs_per_group):
            hp = g * pairs_per_group + pr
            xp = act_ref[:, hp * pair:(hp + 1) * pair]
            xp_bf = xp.astype(BF16)
            xp_t = xp.T.astype(BF16)
            hpair = h_ref[hp * pair:(hp + 1) * pair, :]
            ys, ss = [], []
            for h in (2 * hp, 2 * hp + 1):
                seg = acum[:, h:h + 1] - acum_t[h:h + 1, :]
                wm = jnp.where(causal, jnp.exp(seg), 0.0) * cbm * dt_t[h:h + 1, :]
                ys.append(_dot(wm.astype(BF16), xp_bf))
                bw = (b_g * to_end[:, h:h + 1]).astype(BF16)
                ss.append(_dot(xp_t, bw))
            y_pair = jnp.where(lo_lane, ys[0], ys[1])
            e_pair = jnp.where(lo_lane, eacum[:, 2 * hp:2 * hp + 1], eacum[:, 2 * hp + 1:2 * hp + 2])
            y_pair = y_pair + _dot_nt(c_bf, hpair.astype(BF16)) * e_pair
            s_pair = jnp.where(lo_row, ss[0], ss[1])
            cd_pair = jnp.where(lo_row, cdec_t[2 * hp:2 * hp + 1, :], cdec_t[2 * hp + 1:2 * hp + 2, :])
            h_ref[hp * pair:(hp + 1) * pair, :] = hpair * cd_pair + s_pair
            y_parts.append(y_pair)
        cols = slice(g * grp_w, (g + 1) * grp_w)
        yg = jnp.concatenate(y_parts, axis=1) + dfull_ref[:, cols] * act_ref[:, cols]
        z_g = zx_ref[:, cols]
        yg = yg * (z_g * _sigmoid(z_g))
        ms = jnp.mean(yg * yg, axis=-1, keepdims=True)
        y_ref[:, cols] = (yg * lax.rsqrt(ms + EPS) * nw_ref[:, cols]).astype(y_ref.dtype)


def _ssd(zx, dtr, p, buf0, h0, *, nb, nchunk, tc, t_valid, row_block0):
    d_mix = p["d_mix"]
    wz = zx.shape[1]
    xbc = wz - d_mix
    kw = p["ssd_conv_w"].shape[0]
    hp_rows = h0.shape[1]
    blk = lambda b, c: (row_block0 + b * nchunk + c, 0)
    const = lambda b, c: (0, 0)
    per_b = lambda b, c: (b, 0, 0)
    kern = functools.partial(_ssd_kernel, tc=tc, t_valid=t_valid, d_mix=d_mix, kw=kw)
    return pl.pallas_call(
        kern,
        grid=(nb, nchunk),
        in_specs=[pl.BlockSpec((tc, wz), blk), pl.BlockSpec((tc, LANES), blk),
                  pl.BlockSpec((kw, xbc), const), pl.BlockSpec((1, xbc), const),
                  pl.BlockSpec((1, LANES), const), pl.BlockSpec((1, LANES), const),
                  pl.BlockSpec((1, d_mix), const), pl.BlockSpec((1, d_mix), const),
                  pl.BlockSpec((None, SUBLANES, xbc), per_b), pl.BlockSpec((None, hp_rows, SSD_STATE), per_b)],
        out_specs=[pl.BlockSpec((tc, d_mix), lambda b, c: (b * nchunk + c, 0)),
                   pl.BlockSpec((None, hp_rows, SSD_STATE), per_b)],
        out_shape=[jax.ShapeDtypeStruct((nb * nchunk * tc, d_mix), BF16),
                   jax.ShapeDtypeStruct((nb, hp_rows, SSD_STATE), F32)],
        scratch_shapes=[pltpu.VMEM((tc + SUBLANES, xbc), F32), pltpu.VMEM((tc, xbc), F32)],
        compiler_params=_params(("parallel", "arbitrary")),
    )(zx, dtr, p["ssd_conv_w"], p["ssd_conv_b"], p["ssd_dt_bias"], p["ssd_a_log"], p["ssd_d_full"],
      p["ssd_norm"], buf0, h0)


CONV_HALO = 32


def _conv_kernel(u_ref, buf0_ref, w_ref, b_ref, g_ref, beta_ref, o_ref, ext_ref, acc_ref, *, tt, kw, rb, lb):
    i = pl.program_id(1)
    ch = u_ref.shape[1]

    @pl.when(i == 0)
    def _():
        ext_ref[0:CONV_HALO, :] = buf0_ref[...]

    ext_ref[CONV_HALO:CONV_HALO + tt, :] = u_ref[...]
    base = CONV_HALO - (kw - 1)

    def lane_block(cb, carry):
        cols = pl.ds(pl.multiple_of(cb * lb, lb), lb)
        for rblk in range(tt // rb):
            t0 = rblk * rb
            acc = jnp.broadcast_to(b_ref[:, cols], (rb, lb))
            for r in range(SUBLANES):
                taps = [(a, SUBLANES * a + r - base) for a in range((kw + base) // SUBLANES + 1)
                        if 0 <= SUBLANES * a + r - base < kw]
                win = rb if r == 0 else rb + SUBLANES
                z = None
                for a, k in taps:
                    term = w_ref[k:k + 1, cols] * ext_ref[pl.ds(t0 + SUBLANES * a, win), cols]
                    z = term if z is None else z + term
                acc = acc + (z if r == 0 else pltpu.roll(z, win - r, 0)[:rb])
            acc_ref[pl.ds(t0, rb), cols] = acc
        return carry

    lax.fori_loop(0, ch // lb, lane_block, 0)
    ext_ref[0:CONV_HALO, :] = ext_ref[tt:tt + CONV_HALO, :]

    x = acc_ref[...]
    mu = jnp.mean(x, axis=-1, keepdims=True)
    xc = x - mu
    var = jnp.mean(xc * xc, axis=-1, keepdims=True)
    y = xc * lax.rsqrt(var + EPS) * g_ref[...] + beta_ref[...]
    o_ref[...] = (y * _sigmoid(y)).astype(o_ref.dtype)


def _conv(u, p, buf0, *, nb, ntile, tt, row_block0):
    ch = u.shape[1]
    kw = p["conv_w"].shape[0]
    rb = min(tt, 64)
    lb = LANES
    const = lambda b, i: (0, 0)
    kern = functools.partial(_conv_kernel, tt=tt, kw=kw, rb=rb, lb=lb)
    return pl.pallas_call(
        kern,
        grid=(nb, ntile),
        in_specs=[pl.BlockSpec((tt, ch), lambda b, i: (row_block0 + b * ntile + i, 0)),
                  pl.BlockSpec((None, CONV_HALO, ch), lambda b, i: (b, 0, 0)),
                  pl.BlockSpec((kw, ch), const), pl.BlockSpec((1, ch), const),
                  pl.BlockSpec((1, ch), const), pl.BlockSpec((1, ch), const)],
        out_specs=pl.BlockSpec((tt, ch), lambda b, i: (b * ntile + i, 0)),
        out_shape=jax.ShapeDtypeStruct((nb * ntile * tt, ch), BF16),
        scratch_shapes=[pltpu.VMEM((tt + CONV_HALO, ch), F32), pltpu.VMEM((tt, ch), F32)],
        compiler_params=_params(("parallel", "arbitrary")),
    )(u, buf0, p["conv_w"], p["conv_b"], p["conv_ln_g"], p["conv_ln_b"])


def _moba_prompt_kernel(q_ref, k_ref, v_ref, o_ref, kmean_ref, kb_ref, vt_ref, qt_ref, sel_ref, acc_ref,
                        *, nblk, hpg, scale):
    i = pl.program_id(2)
    blk = MOBA_BLOCK
    hd = ATT_HEAD_DIM
    nq = hpg * blk
    nch = nq // LANES

    @pl.when(i == 0)
    def _():
        kmean_ref[...] = jnp.mean(k_ref[...].reshape(nblk, blk, hd), axis=1)
        for jb in range(nblk):
            kb_ref[jb] = k_ref[jb * blk:(jb + 1) * blk, :].astype(BF16)
            vt_ref[jb] = v_ref[jb * blk:(jb + 1) * blk, :].T.astype(BF16)

    qt = jnp.concatenate([q_ref[:, hh * hd:(hh + 1) * hd].T for hh in range(hpg)], axis=1)
    qt_ref[...] = qt.astype(BF16)
    gate = jnp.dot(kmean_ref[...], qt, preferred_element_type=F32, precision=lax.Precision.HIGHEST)
    blk_row = lax.broadcasted_iota(jnp.int32, (nblk, nq), 0)
    gate = jnp.where(blk_row < i, gate, NEG_INF)
    rank = jnp.zeros((nblk, nq), jnp.int32)
    for jp in range(nblk):
        gj = gate[jp:jp + 1, :]
        rank = rank + jnp.where(gj > gate, 1, jnp.where((gj == gate) & (blk_row > jp), 1, 0))
    sel = jnp.where((rank < MOBA_TOPK) & (gate > NEG_INF), 1.0, 0.0)
    for jb in range(nblk):
        sel_ref[jb] = jnp.broadcast_to(sel[jb:jb + 1, :], (SUBLANES, nq))

    key_i = lax.broadcasted_iota(jnp.int32, (blk, LANES), 0)
    lane_i = lax.broadcasted_iota(jnp.int32, (blk, LANES), 1)

    def block_update(j, m, l, diagonal):
        kj = kb_ref[j]
        vtj = vt_ref[j]
        m_out, l_out = [], []
        for c in range(nch):
            cs = slice(c * LANES, (c + 1) * LANES)
            s = _dot(kj, qt_ref[:, cs]) * scale
            if diagonal:
                q_off = (c * LANES) % blk
                s = jnp.where(key_i <= lane_i + q_off, s, NEG_INF)
                m_new = jnp.max(s, axis=0, keepdims=True)
                p = jnp.exp(s - m_new)
                l_out.append(jnp.sum(p, axis=0, keepdims=True))
                acc_ref[:, cs] = _dot(vtj, p.astype(BF16))
            else:
                s = jnp.where(sel_ref[j, 0:1, cs] > 0.0, s, NEG_INF)
                m_old = m[:, cs]
                m_new = jnp.maximum(m_old, jnp.max(s, axis=0, keepdims=True))
                alpha = jnp.exp(m_old - m_new)
                p = jnp.exp(s - m_new)
                l_out.append(alpha * l[:, cs] + jnp.sum(p, axis=0, keepdims=True))
                acc_ref[:, cs] = alpha * acc_ref[:, cs] + _dot(vtj, p.astype(BF16))
            m_out.append(m_new)
        return jnp.concatenate(m_out, axis=1), jnp.concatenate(l_out, axis=1)

    m0, l0 = block_update(i, None, None, True)
    m, l = lax.fori_loop(0, i, lambda j, ml: block_update(j, ml[0], ml[1], False), (m0, l0))
    out_t = acc_ref[...] / l
    for hh in range(hpg):
        o_ref[:, hh * hd:(hh + 1) * hd] = out_t[:, hh * blk:(hh + 1) * blk].T.astype(o_ref.dtype)


def _moba_prompt(qkv, *, nb, t_len, n_heads, kv_heads):
    hd = ATT_HEAD_DIM
    blk = MOBA_BLOCK
    assert t_len % blk == 0
    nblk = t_len // blk
    hpg = n_heads // kv_heads
    kcol0 = n_heads
    vcol0 = n_heads + kv_heads
    kern = functools.partial(_moba_prompt_kernel, nblk=nblk, hpg=hpg, scale=hd ** -0.5)
    return pl.pallas_call(
        kern,
        grid=(nb, kv_heads, nblk),
        in_specs=[pl.BlockSpec((blk, hpg * hd), lambda b, g, i: (b * nblk + i, g)),
                  pl.BlockSpec((t_len, hd), lambda b, g, i: (b, kcol0 + g)),
                  pl.BlockSpec((t_len, hd), lambda b, g, i: (b, vcol0 + g))],
        out_specs=pl.BlockSpec((blk, hpg * hd), lambda b, g, i: (b * nblk + i, g)),
        out_shape=jax.ShapeDtypeStruct((nb * t_len, n_heads * hd), BF16),
        scratch_shapes=[pltpu.VMEM((nblk, hd), F32), pltpu.VMEM((nblk, blk, hd), BF16),
                        pltpu.VMEM((nblk, hd, blk), BF16), pltpu.VMEM((hd, hpg * blk), BF16),
                        pltpu.VMEM((nblk, SUBLANES, hpg * blk), F32), pltpu.VMEM((hd, hpg * blk), F32)],
        compiler_params=_params(("parallel", "parallel", "arbitrary")),
    )(qkv, qkv, qkv)


def _moba_sample_kernel(pt_ref, q_ref, *refs, nblk, bps, ppb, kv_heads, rows_per_group, dt_len, scale):
    del pt_ref
    npg = bps * ppb
    kpages, vpages = refs[:npg], refs[npg:2 * npg]
    knew_ref, vnew_ref, o_ref, gate_ref, m_ref, l_ref, acc_ref = refs[2 * npg:]
    step = pl.program_id(1)
    hd = ATT_HEAD_DIM
    n_rows = q_ref.shape[0]
    rpg = rows_per_group
    n_keys = MOBA_BLOCK * kv_heads
    q = q_ref[...]
    lane = lax.broadcasted_iota(jnp.int32, (n_rows, LANES), 1)

    @pl.when(step == 0)
    def _():
        gate_ref[...] = jnp.full((n_rows, LANES), NEG_INF, F32)
        m_ref[...] = jnp.zeros((n_rows, LANES), F32)
        l_ref[...] = jnp.zeros((n_rows, LANES), F32)

    key_head = lax.broadcasted_iota(jnp.int32, (n_rows, n_keys), 1) % kv_heads
    row_head = lax.broadcasted_iota(jnp.int32, (n_rows, n_keys), 0) // rpg
    own_head = key_head == row_head
    g_lane = lax.broadcasted_iota(jnp.int32, (n_rows, SUBLANES), 1) % kv_heads
    g_rowh = lax.broadcasted_iota(jnp.int32, (n_rows, SUBLANES), 0) // rpg
    q_bf = q.astype(BF16)
    gate_new, m_new, l_new = gate_ref[...], m_ref[...], l_ref[...]
    for i in range(bps):
        j = step * bps + i
        kblk = jnp.concatenate([r[...] for r in kpages[i * ppb:(i + 1) * ppb]], axis=0)
        vblk = jnp.concatenate([r[...] for r in vpages[i * ppb:(i + 1) * ppb]], axis=0)
        s = _dot_nt(q_bf, kblk.astype(BF16)) * scale
        s = jnp.where(own_head, s, NEG_INF)
        m_col = jnp.max(s, axis=-1, keepdims=True)
        p = jnp.exp(s - m_col)
        l_col = jnp.sum(p, axis=-1, keepdims=True)
        acc_ref[j] = _dot(p.astype(BF16), vblk.astype(BF16))
        ksub = jnp.sum(kblk.reshape(n_keys // SUBLANES, SUBLANES, hd), axis=0) * (1.0 / MOBA_BLOCK)
        gall = _dot_nt(q, ksub, precision=lax.Precision.HIGHEST)
        g_col = jnp.sum(jnp.where(g_lane == g_rowh, gall, 0.0), axis=-1, keepdims=True)
        is_j = lane == j
        gate_new = jnp.where(is_j, g_col, gate_new)
        m_new = jnp.where(is_j, m_col, m_new)
        l_new = jnp.where(is_j, l_col, l_new)
    gate_ref[...] = gate_new
    m_ref[...] = m_new
    l_ref[...] = l_new

    def group_partials(g, kg, vg, mask):
        qg = q_ref[g * rpg:(g + 1) * rpg, :]
        sg = jnp.where(mask, _dot_nt(qg.astype(BF16), kg.astype(BF16)) * scale, NEG_INF)
        mg = jnp.max(sg, axis=-1, keepdims=True)
        pg = jnp.exp(sg - mg)
        return mg, jnp.sum(pg, axis=-1, keepdims=True), _dot(pg.astype(BF16), vg.astype(BF16))

    @pl.when(step == nblk // bps - 1)
    def _():
        work = gate_ref[...]
        sel = jnp.zeros((n_rows, LANES), F32)
        for _ in range(min(MOBA_TOPK, nblk)):
            mx = jnp.max(work, axis=-1, keepdims=True)
            idx = jnp.min(jnp.where(work == mx, lane, LANES), axis=-1, keepdims=True)
            pick = lane == idx
            sel = jnp.where(pick & (mx > NEG_INF), 1.0, sel)
            work = jnp.where(pick, NEG_INF, work)
        row_t = lax.broadcasted_iota(jnp.int32, (rpg, knew_ref.shape[0]), 0) % dt_len
        key_i = lax.broadcasted_iota(jnp.int32, (rpg, knew_ref.shape[0]), 1)
        own = [group_partials(g, knew_ref[:, g * hd:(g + 1) * hd], vnew_ref[:, g * hd:(g + 1) * hd],
                              key_i <= row_t) for g in range(kv_heads)]
        m_own = jnp.concatenate([o[0] for o in own], axis=0)
        l_own = jnp.concatenate([o[1] for o in own], axis=0)
        acc_own = jnp.concatenate([o[2] for o in own], axis=0)
        m_all = m_ref[...]
        picked = sel > 0.0
        m_tot = jnp.maximum(jnp.max(jnp.where(picked, m_all, NEG_INF), axis=-1, keepdims=True), m_own)
        w = jnp.where(picked, jnp.exp(m_all - m_tot), 0.0)
        w_own = jnp.exp(m_own - m_tot)
        l_tot = jnp.sum(w * l_ref[...], axis=-1, keepdims=True) + w_own * l_own
        out = w_own * acc_own
        for jj in range(nblk):
            out = out + w[:, jj:jj + 1] * acc_ref[jj]
        o_ref[...] = out / l_tot


def _moba_sample(q_rows, knew, vnew, cache_k, cache_v, page_table, layer, *, kv_heads, dt_len):
    db, n_rows, hd = q_rows.shape
    page_rows = cache_k.shape[2]
    page = page_rows // kv_heads
    width = kv_heads * hd
    ppb = MOBA_BLOCK // page
    n_pages = page_table.shape[1]
    assert n_pages % ppb == 0 and MOBA_BLOCK % page == 0 and SUBLANES % kv_heads == 0
    nblk = n_pages // ppb
    assert nblk <= LANES
    rows_per_group = n_rows // kv_heads

    bps = _pick_tile(nblk, 4, 1)
    pps = bps * ppb

    def page_spec(o):
        return pl.BlockSpec((None, None, page_rows, hd), lambda b, j, pt: (layer, pt[b, pps * j + o], 0, 0))

    page_specs = [page_spec(o) for o in range(pps)]
    per_b = lambda b, j, pt: (b, 0, 0)
    kern = functools.partial(_moba_sample_kernel, nblk=nblk, bps=bps, ppb=ppb, kv_heads=kv_heads,
                             rows_per_group=rows_per_group, dt_len=dt_len, scale=hd ** -0.5)
    grid_spec = pltpu.PrefetchScalarGridSpec(
        num_scalar_prefetch=1,
        grid=(db, nblk // bps),
        in_specs=[pl.BlockSpec((None, n_rows, hd), per_b)]
                 + page_specs + page_specs
                 + [pl.BlockSpec((None, knew.shape[1], width), per_b)] * 2,
        out_specs=pl.BlockSpec((None, n_rows, hd), per_b),
        scratch_shapes=[pltpu.VMEM((n_rows, LANES), F32)] * 3 + [pltpu.VMEM((nblk, n_rows, hd), F32)],
    )
    return pl.pallas_call(
        kern,
        grid_spec=grid_spec,
        out_shape=jax.ShapeDtypeStruct((db, n_rows, hd), F32),
        compiler_params=_params(("parallel", "arbitrary")),
    )(page_table, q_rows, *([cache_k] * len(page_specs)), *([cache_v] * len(page_specs)), knew, vnew)


def _rope_tables(pos):
    half = ATT_HEAD_DIM // 2
    inv_freq = jnp.exp(-math.log(ROPE_THETA) * jnp.arange(half, dtype=F32) * 2.0 / ATT_HEAD_DIM)
    ang = pos[:, None] * inv_freq[None, :]
    cos, sin = jnp.cos(ang), jnp.sin(ang)
    return jnp.concatenate([cos, cos], axis=-1), jnp.concatenate([-sin, sin], axis=-1)


def _prep_weights(d_model, w_in, w_ssd_out, w_conv_out, w_att_out, w_o, w_up, w_down, conv_ch, n_heads, kv_heads):
    d_mix = d_model // 2
    ssd_heads = d_mix // SSD_HEADDIM
    xbc = d_mix + 2 * SSD_GROUPS * SSD_STATE
    o_dt = d_mix + xbc
    o_conv = o_dt + ssd_heads
    o_q = o_conv + 2 * conv_ch
    o_g = o_q + (n_heads + 2 * kv_heads) * ATT_HEAD_DIM
    return {
        "w_zx": w_in[:, :, :o_dt].astype(BF16),
        "w_dt": jnp.pad(w_in[:, :, o_dt:o_conv], ((0, 0), (0, 0), (0, LANES - ssd_heads))).astype(BF16),
        "w_val": w_in[:, :, o_conv:o_conv + conv_ch].astype(BF16),
        "w_gate": w_in[:, :, o_conv + conv_ch:o_q].astype(BF16),
        "w_qkv": w_in[:, :, o_q:o_g].astype(BF16),
        "w_g": w_in[:, :, o_g:].astype(BF16),
        "w_branch": jnp.stack([w_ssd_out, w_conv_out, w_att_out]).astype(BF16),
        "w_o": w_o.astype(BF16), "w_up": w_up.astype(BF16), "w_down": w_down.astype(BF16),
    }


def _prep_layer(l, d_mix, ssd_conv_w, ssd_conv_b, ssd_dt_bias, ssd_a_log, ssd_d, ssd_norm,
                conv_w, conv_b, conv_ln_g, conv_ln_b, norm_mix, norm_ffn):
    xbc = ssd_conv_w.shape[2]
    conv_ch = conv_w.shape[2]
    pad_row = lambda v: jnp.pad(v, (0, LANES - v.shape[0])).reshape(1, LANES)
    return {
        "d_mix": d_mix,
        "norm_mix": norm_mix[l], "norm_ffn": norm_ffn[l],
        "ssd_conv_w": ssd_conv_w[l], "ssd_conv_b": ssd_conv_b[l].reshape(1, xbc),
        "ssd_dt_bias": pad_row(ssd_dt_bias[l]), "ssd_a_log": pad_row(ssd_a_log[l]),
        "ssd_d_full": jnp.repeat(ssd_d[l], SSD_HEADDIM).reshape(1, d_mix),
        "ssd_norm": ssd_norm[l].reshape(1, d_mix),
        "conv_w": conv_w[l], "conv_b": conv_b[l].reshape(1, conv_ch),
        "conv_ln_g": conv_ln_g[l].reshape(1, conv_ch), "conv_ln_b": conv_ln_b[l].reshape(1, conv_ch),
    }


def _last_rows(a, nb, t_len, n, col0=0):
    return jnp.stack([a[(b + 1) * t_len - n:(b + 1) * t_len, col0:] for b in range(nb)])


def _layer(x, p, w, l, cos_t, sin_t, cache_k, cache_v, page_table, state_ssm_l, state_ssm_conv_l, state_conv_l,
           *, nbp, t_len, dbs, dt_len, n_heads, kv_heads):
    m, d_model = x.shape
    d_mix = p["d_mix"]
    hd = ATT_HEAD_DIM
    bt = nbp * t_len
    xbc_w = p["ssd_conv_w"].shape[1]
    conv_ch = p["conv_w"].shape[1]
    kv_w = kv_heads * hd

    h = _rmsnorm(x, p["norm_mix"], BF16)
    zx = _matmul(h, w["w_zx"], l)
    dtr = _matmul(h, w["w_dt"], l, tn=LANES)
    u = _matmul_glu(h, w["w_val"], w["w_gate"], l)
    n_rope_tiles = (n_heads + kv_heads) * hd // 512
    qkv = _matmul(h, w["w_qkv"], l, epilogue=functools.partial(_ep_rope, n_rope_tiles=n_rope_tiles),
                  extra=(cos_t, sin_t), extra_specs=(_spec_row_table, _spec_row_table))
    gates = _matmul(h, w["w_g"], l, epilogue=_ep_sigmoid, out_dtype=BF16)

    tc = 128
    ncp = t_len // tc
    y_p, ssm_p = _ssd(zx, dtr, p, jnp.zeros((nbp, SUBLANES, xbc_w), F32),
                      jnp.zeros((nbp, d_mix, SSD_STATE), F32),
                      nb=nbp, nchunk=ncp, tc=tc, t_valid=tc, row_block0=0)
    zx_s = zx[bt:].reshape(dbs, dt_len, -1)
    pad_t = ((0, 0), (0, tc - dt_len), (0, 0))
    zx_sp = jnp.pad(zx_s, pad_t).reshape(dbs * tc, -1)
    dtr_sp = jnp.pad(dtr[bt:].reshape(dbs, dt_len, LANES), pad_t).reshape(dbs * tc, LANES)
    kc = state_ssm_conv_l.shape[1]
    buf0_s = jnp.pad(state_ssm_conv_l, ((0, 0), (SUBLANES - kc, 0), (0, 0)))
    y_s, ssm_s = _ssd(zx_sp, dtr_sp, p, buf0_s, state_ssm_l.reshape(dbs, d_mix, SSD_STATE),
                      nb=dbs, nchunk=1, tc=tc, t_valid=dt_len, row_block0=0)
    y_s = y_s.reshape(dbs, tc, d_mix)[:, :dt_len].reshape(dbs * dt_len, d_mix)
    ssm_buf_p = _last_rows(zx, nbp, t_len, kc, d_mix)
    ssm_buf_s = jnp.concatenate([state_ssm_conv_l, zx_s[:, :, d_mix:]], axis=1)[:, -kc:]

    tt = 256
    cw1 = p["conv_w"].shape[0] - 1
    c_p = _conv(u, p, jnp.zeros((nbp, CONV_HALO, conv_ch), F32), nb=nbp, ntile=t_len // tt, tt=tt, row_block0=0)
    buf0_c = jnp.pad(state_conv_l, ((0, 0), (CONV_HALO - cw1, 0), (0, 0)))
    c_s = _conv(u, p, buf0_c, nb=dbs, ntile=1, tt=dt_len, row_block0=bt // dt_len)
    conv_buf_p = _last_rows(u, nbp, t_len, cw1)
    conv_buf_s = jnp.concatenate([state_conv_l, u[bt:].reshape(dbs, dt_len, conv_ch)], axis=1)[:, -cw1:]

    a_p = _moba_prompt(qkv, nb=nbp, t_len=t_len, n_heads=n_heads, kv_heads=kv_heads)
    qkv_s = qkv[bt:]
    q_rows = qkv_s[:, :n_heads * hd].reshape(dbs, dt_len, n_heads, hd).transpose(0, 2, 1, 3)
    q_rows = q_rows.reshape(dbs, n_heads * dt_len, hd)
    k_s = qkv_s[:, n_heads * hd:n_heads * hd + kv_w].reshape(dbs, dt_len, kv_w)
    v_s = qkv_s[:, n_heads * hd + kv_w:].reshape(dbs, dt_len, kv_w)
    pad_k = ((0, 0), (0, LANES - dt_len), (0, 0))
    a_s = _moba_sample(q_rows, jnp.pad(k_s, pad_k), jnp.pad(v_s, pad_k), cache_k, cache_v, page_table, l,
                       kv_heads=kv_heads, dt_len=dt_len)
    a_s = a_s.reshape(dbs, n_heads, dt_len, hd).transpose(0, 2, 1, 3).reshape(dbs * dt_len, n_heads * hd)
    k_p = qkv[:bt, n_heads * hd:n_heads * hd + kv_w].reshape(nbp, t_len, kv_heads, hd)
    v_p = qkv[:bt, n_heads * hd + kv_w:].reshape(nbp, t_len, kv_heads, hd)

    y3 = jnp.stack([jnp.concatenate([y_p, y_s]), jnp.concatenate([c_p, c_s]),
                    jnp.concatenate([a_p, a_s.astype(BF16)])])
    merged = _merge(y3, w["w_branch"], gates, l)
    res_spec = (_spec_out_tile,)
    x = _matmul(merged, w["w_o"], l, epilogue=_ep_residual, extra=(x,), extra_specs=res_spec)
    hf = _rmsnorm(x, p["norm_ffn"], BF16)
    act = _matmul(hf, w["w_up"], l, epilogue=_ep_relu2, out_dtype=BF16)
    x = _matmul(act, w["w_down"], l, epilogue=_ep_residual, extra=(x,), extra_specs=res_spec, tk=4096)
    states = (k_p, v_p, k_s.reshape(dbs, dt_len, kv_heads, hd), v_s.reshape(dbs, dt_len, kv_heads, hd),
              ssm_p.reshape(nbp, -1, SSD_HEADDIM, SSD_STATE), ssm_s.reshape(dbs, -1, SSD_HEADDIM, SSD_STATE),
              ssm_buf_p, ssm_buf_s, conv_buf_p, conv_buf_s)
    return x, states


def kernel(x_prompt, x_sample, cache_k, cache_v, page_table, state_ssm, state_ssm_conv, state_conv, norm_mix, w_in, ssd_conv_w, ssd_conv_b, ssd_dt_bias, ssd_a_log, ssd_d, ssd_norm, w_ssd_out, conv_w, conv_b, conv_ln_g, conv_ln_b, w_conv_out, w_att_out, w_o, norm_ffn, w_up, w_down, norm_final):
    nbp, t_len, d_model = x_prompt.shape
    dbs, dt_len, _ = x_sample.shape
    depth = w_in.shape[0]
    n_phys, page, kv_heads, hd = cache_k.shape[1:]
    assert hd == ATT_HEAD_DIM
    n_heads = w_att_out.shape[1] // hd
    past_len = page_table.shape[1] * page
    assert past_len % MOBA_BLOCK == 0 and dt_len <= SUBLANES
    bt = nbp * t_len

    x = jnp.concatenate([x_prompt.reshape(bt, d_model), x_sample.reshape(dbs * dt_len, d_model)], axis=0)
    pos = jnp.concatenate([jnp.tile(jnp.arange(t_len, dtype=F32), nbp),
                           jnp.tile(past_len + jnp.arange(dt_len, dtype=F32), dbs)])
    cos_t, sin_t = _rope_tables(pos)
    ck = cache_k.reshape(depth, n_phys, page * kv_heads, hd)
    cv = cache_v.reshape(depth, n_phys, page * kv_heads, hd)
    w = _prep_weights(d_model, w_in, w_ssd_out, w_conv_out, w_att_out, w_o, w_up, w_down, conv_w.shape[2],
                      n_heads, kv_heads)

    per_layer = []
    for l in range(depth):
        p = _prep_layer(l, d_model // 2, ssd_conv_w, ssd_conv_b, ssd_dt_bias, ssd_a_log, ssd_d, ssd_norm,
                        conv_w, conv_b, conv_ln_g, conv_ln_b, norm_mix, norm_ffn)
        x, st = _layer(x, p, w, l, cos_t, sin_t, ck, cv, page_table, state_ssm[l], state_ssm_conv[l],
                       state_conv[l], nbp=nbp, t_len=t_len, dbs=dbs, dt_len=dt_len, n_heads=n_heads,
                       kv_heads=kv_heads)
        per_layer.append(st)
    y_prompt = _rmsnorm(x, norm_final, F32, 0, bt).reshape(nbp, t_len, d_model)
    y_sample = _rmsnorm(x, norm_final, F32, bt, dbs * dt_len).reshape(dbs, dt_len, d_model)
    stacked = [jnp.stack([st[i] for st in per_layer]) for i in range(10)]
    return (y_prompt, y_sample, *stacked)
```

```python
import functools
import math

import jax
import jax.numpy as jnp
from jax import lax
from jax.experimental import pallas as pl
from jax.experimental.pallas import tpu as pltpu

F32 = jnp.float32
BF16 = jnp.bfloat16

SSD_HEADDIM = 64
SSD_STATE = 128
SSD_GROUPS = 4
ATT_HEAD_DIM = 128
MOBA_BLOCK = 256
MOBA_TOPK = 3
ROPE_THETA = 10000.0
EPS = 1e-6

LANES = 128
SUBLANES = 8
BF16_ROWS = 16
VMEM_LIMIT = 56 * 1024 * 1024

NEG_INF = float("-inf")


def _pick_tile(n, cap, mult):
    best = None
    for d in range(mult, min(n, cap) + 1, mult):
        if n % d == 0:
            best = d
    return n if best is None else best


def _params(sem):
    return pltpu.CompilerParams(dimension_semantics=sem, vmem_limit_bytes=VMEM_LIMIT)


def _sigmoid(x):
    return jax.nn.sigmoid(x)


def _dot(a, b):
    return jnp.dot(a, b, preferred_element_type=F32)


def _dot_nt(a, b, precision=None):
    return lax.dot_general(a, b, (((1,), (1,)), ((), ())), preferred_element_type=F32,
                           precision=precision)


def _rmsnorm_kernel(x_ref, g_ref, o_ref):
    x = x_ref[...]
    ms = jnp.mean(x * x, axis=-1, keepdims=True)
    o_ref[...] = (x * lax.rsqrt(ms + EPS) * g_ref[...]).astype(o_ref.dtype)


def _rmsnorm(x, g, out_dtype, row0=0, rows=None):
    d = x.shape[1]
    rows = x.shape[0] if rows is None else rows
    tr = _pick_tile(math.gcd(rows, row0) if row0 else rows, 256, 16)
    blk0 = row0 // tr
    return pl.pallas_call(
        _rmsnorm_kernel,
        grid=(rows // tr,),
        in_specs=[pl.BlockSpec((tr, d), lambda i: (blk0 + i, 0)), pl.BlockSpec((1, d), lambda i: (0, 0))],
        out_specs=pl.BlockSpec((tr, d), lambda i: (i, 0)),
        out_shape=jax.ShapeDtypeStruct((rows, d), out_dtype),
        compiler_params=_params(("parallel",)),
    )(x, g.reshape(1, d))


def _ep_store(acc, o_ref):
    o_ref[...] = acc.astype(o_ref.dtype)


def _ep_sigmoid(acc, o_ref):
    o_ref[...] = _sigmoid(acc).astype(o_ref.dtype)


def _ep_relu2(acc, o_ref):
    r = jnp.maximum(acc, 0.0)
    o_ref[...] = (r * r).astype(o_ref.dtype)


def _ep_residual(acc, o_ref, res_ref):
    o_ref[...] = (res_ref[...] + acc).astype(o_ref.dtype)


def _ep_rope(acc, o_ref, cos_ref, sin_ref, *, n_rope_tiles):
    j = pl.program_id(1)
    c = cos_ref[...]
    s = sin_ref[...]
    do_rope = j < n_rope_tiles
    for h in range(acc.shape[1] // ATT_HEAD_DIM):
        a = acc[:, h * ATT_HEAD_DIM:(h + 1) * ATT_HEAD_DIM]
        partner = pltpu.roll(a, ATT_HEAD_DIM // 2, 1)
        o_ref[:, h * ATT_HEAD_DIM:(h + 1) * ATT_HEAD_DIM] = jnp.where(do_rope, a * c + partner * s, a)


def _row_chunks(tm):
    if tm % (2 * BF16_ROWS):
        return [(0, tm)]
    return [(0, tm // 2), (tm // 2, tm // 2)]


CAST_K = 1024


def _x_dot_w(x_ref, rows, w_ref):
    if w_ref.dtype == BF16:
        return _dot(x_ref[rows, :], w_ref[...])
    acc = None
    for k0 in range(0, w_ref.shape[0], CAST_K):
        part = _dot(x_ref[rows, k0:k0 + CAST_K], w_ref[k0:k0 + CAST_K, :].astype(BF16))
        acc = part if acc is None else acc + part
    return acc


def _mm_kernel(*refs, nk, n_extra, epilogue, split_rows):
    x_ref, w_ref = refs[0], refs[1]
    extra = refs[2:2 + n_extra]
    o_ref = refs[2 + n_extra]
    tm = x_ref.shape[0]
    if nk == 1:
        for r0, rc in (_row_chunks(tm) if split_rows else [(0, tm)]):
            rows = pl.ds(r0, rc)
            epilogue(_x_dot_w(x_ref, rows, w_ref), o_ref.at[rows], *[e.at[rows] for e in extra])
        return
    acc_ref = refs[3 + n_extra]
    k = pl.program_id(2)
    all_rows = pl.ds(0, tm)

    @pl.when(k == 0)
    def _():
        acc_ref[...] = _x_dot_w(x_ref, all_rows, w_ref)

    @pl.when(k > 0)
    def _():
        acc_ref[...] += _x_dot_w(x_ref, all_rows, w_ref)

    @pl.when(k == nk - 1)
    def _():
        epilogue(acc_ref[...], o_ref, *extra)


def _matmul(x, w, layer, *, epilogue=_ep_store, out_dtype=F32, tm_cap=1376, tn=512, tk=None,
            extra=(), extra_specs=(), split_rows=False, col0=0, n=None):
    m, kdim = x.shape
    n = w.shape[2] - col0 if n is None else n
    tm = _pick_tile(m, tm_cap, 16)
    tn = min(tn, n)
    tk = kdim if tk is None else tk
    assert n % tn == 0 and kdim % tk == 0 and col0 % tn == 0 and col0 + n <= w.shape[2]
    cb0 = col0 // tn
    nk = kdim // tk
    grid = (m // tm, n // tn, nk)
    kern = functools.partial(_mm_kernel, nk=nk, n_extra=len(extra), epilogue=epilogue, split_rows=split_rows)
    scratch = [] if nk == 1 else [pltpu.VMEM((tm, tn), F32)]
    return pl.pallas_call(
        kern,
        grid=grid,
        in_specs=[pl.BlockSpec((tm, tk), lambda i, j, k: (i, k)),
                  pl.BlockSpec((None, tk, tn), lambda i, j, k: (layer, k, cb0 + j))]
                 + [s(tm, tn) for s in extra_specs],
        out_specs=pl.BlockSpec((tm, tn), lambda i, j, k: (i, j)),
        out_shape=jax.ShapeDtypeStruct((m, n), out_dtype),
        scratch_shapes=scratch,
        compiler_params=_params(("parallel", "parallel", "arbitrary")),
    )(x, w, *extra)


def _glu_kernel(x_ref, wv_ref, wg_ref, o_ref):
    x = x_ref[...]
    o_ref[...] = (_dot(x, wv_ref[...]) * _sigmoid(_dot(x, wg_ref[...]))).astype(o_ref.dtype)


def _matmul_glu(x, w, layer, col_val, col_gate, n, *, tm_cap=1376, tn=256):
    m, kdim = x.shape
    tm = _pick_tile(m, tm_cap, 16)
    assert n % tn == 0 and col_val % tn == 0 and col_gate % tn == 0
    bv, bg = col_val // tn, col_gate // tn
    return pl.pallas_call(
        _glu_kernel,
        grid=(m // tm, n // tn),
        in_specs=[pl.BlockSpec((tm, kdim), lambda i, j: (i, 0)),
                  pl.BlockSpec((None, kdim, tn), lambda i, j: (layer, 0, bv + j)),
                  pl.BlockSpec((None, kdim, tn), lambda i, j: (layer, 0, bg + j))],
        out_specs=pl.BlockSpec((tm, tn), lambda i, j: (i, j)),
        out_shape=jax.ShapeDtypeStruct((m, n), F32),
        compiler_params=_params(("parallel", "parallel")),
    )(x, w, w)


def _spec_row_table(tm, tn):
    return pl.BlockSpec((tm, ATT_HEAD_DIM), lambda i, j, k: (i, 0))


def _spec_out_tile(tm, tn):
    return pl.BlockSpec((tm, tn), lambda i, j, k: (i, j))


def _cast_tail_kernel(w_ref, o_ref, *, col0):
    o_ref[...] = w_ref[:, col0:].astype(o_ref.dtype)


def _cast_tail(w, col0, rows=128):
    nl, kdim, n = w.shape
    assert kdim % rows == 0 and (n - col0) % LANES == 0
    return pl.pallas_call(
        functools.partial(_cast_tail_kernel, col0=col0),
        grid=(nl, kdim // rows),
        in_specs=[pl.BlockSpec((None, rows, n), lambda l, i: (l, i, 0))],
        out_specs=pl.BlockSpec((None, rows, n - col0), lambda l, i: (l, i, 0)),
        out_shape=jax.ShapeDtypeStruct((nl, kdim, n - col0), BF16),
        compiler_params=_params(("parallel", "parallel")),
    )(w)


def _merge_kernel(y_ref, w_ref, g_ref, o_ref, tot_ref):
    b = pl.program_id(2)
    chunks = _row_chunks(y_ref.shape[0])

    def gated(rows):
        return g_ref[rows, :].astype(F32) * _dot(y_ref[rows, :], w_ref[...])

    @pl.when(b == 0)
    def _():
        for r0, rc in chunks:
            rows = pl.ds(r0, rc)
            tot_ref[rows, :] = gated(rows)

    @pl.when(b > 0)
    def _():
        for r0, rc in chunks:
            rows = pl.ds(r0, rc)
            tot = tot_ref[rows, :] + gated(rows)
            tot_ref[rows, :] = tot
            o_ref[rows, :] = tot.astype(o_ref.dtype)


def _merge(y3, w3, g, layer, tm_cap=1376, tn=1024):
    nb, m, kb = y3.shape
    d = w3.shape[3]
    tm = _pick_tile(m, tm_cap, 16)
    ncol = d // tn
    return pl.pallas_call(
        _merge_kernel,
        grid=(m // tm, ncol, nb),
        in_specs=[pl.BlockSpec((None, tm, kb), lambda i, j, b: (b, i, 0)),
                  pl.BlockSpec((None, None, kb, tn), lambda i, j, b: (b, layer, 0, j)),
                  pl.BlockSpec((tm, tn), lambda i, j, b: (i, b * ncol + j))],
        out_specs=pl.BlockSpec((tm, tn), lambda i, j, b: (i, j)),
        out_shape=jax.ShapeDtypeStruct((m, d), BF16),
        scratch_shapes=[pltpu.VMEM((tm, tn), F32)],
        compiler_params=_params(("parallel", "parallel", "arbitrary")),
    )(y3, w3, g)


def _ssd_kernel(zx_ref, dt_ref, cw_ref, cb_ref, dtb_ref, alog_ref, dfull_ref, nw_ref, buf0_ref, h0_ref,
                y_ref, h_ref, ext_ref, act_ref, *, tc, t_valid, d_mix, kw):
    c = pl.program_id(1)
    n_state = SSD_STATE
    pair = 2 * SSD_HEADDIM
    grp_w = d_mix // SSD_GROUPS
    pairs_per_group = grp_w // pair

    @pl.when(c == 0)
    def _():
        h_ref[...] = h0_ref[...]
        ext_ref[0:SUBLANES, :] = buf0_ref[...]

    ext_ref[SUBLANES:SUBLANES + tc, :] = zx_ref[:, d_mix:]
    conv = cb_ref[...] + cw_ref[0:1, :] * ext_ref[SUBLANES - (kw - 1):SUBLANES - (kw - 1) + tc, :]
    for k in range(1, kw):
        off = SUBLANES - (kw - 1) + k
        conv = conv + cw_ref[k:k + 1, :] * ext_ref[off:off + tc, :]
    ext_ref[0:SUBLANES, :] = ext_ref[tc:tc + SUBLANES, :]
    act_ref[...] = conv * _sigmoid(conv)

    dt = jax.nn.softplus(dt_ref[...] + dtb_ref[...])
    row = lax.broadcasted_iota(jnp.int32, (tc, LANES), 0)
    if t_valid < tc:
        dt = jnp.where(row < t_valid, dt, 0.0)
    a = -jnp.exp(alog_ref[...])
    da = dt * a
    r_i = lax.broadcasted_iota(jnp.int32, (tc, tc), 0)
    c_i = lax.broadcasted_iota(jnp.int32, (tc, tc), 1)
    causal = c_i <= r_i
    tri = jnp.where(causal, 1.0, 0.0).astype(BF16)
    da_hi = da.astype(BF16)
    rem = da - da_hi.astype(F32)
    da_mid = rem.astype(BF16)
    da_lo = (rem - da_mid.astype(F32)).astype(BF16)
    acum = _dot(tri, da_hi) + _dot(tri, da_mid) + _dot(tri, da_lo)
    acum_t = acum.T
    dt_t = dt.T
    a_tot = acum[tc - 1:tc, :]
    to_end = jnp.exp(a_tot - acum) * dt
    eacum = jnp.exp(acum)
    cdec_t = jnp.exp(acum_t[:, tc - 1:tc])

    lane = lax.broadcasted_iota(jnp.int32, (tc, pair), 1)
    lo_lane = lane < SSD_HEADDIM
    prow = lax.broadcasted_iota(jnp.int32, (pair, n_state), 0)
    lo_row = prow < SSD_HEADDIM

    for g in range(SSD_GROUPS):
        b_g = act_ref[:, d_mix + g * n_state:d_mix + (g + 1) * n_state]
        c_g = act_ref[:, d_mix + (SSD_GROUPS + g) * n_state:d_mix + (SSD_GROUPS + g + 1) * n_state]
        c_bf = c_g.astype(BF16)
        cbm = _dot_nt(c_bf, b_g.astype(BF16))
        y_parts = []
        for pr in range(pairs_per_group):
            hp = g * pairs_per_group + pr
            xp = act_ref[:, hp * pair:(hp + 1) * pair]
            xp_bf = xp.astype(BF16)
            xp_t = xp.T.astype(BF16)
            hpair = h_ref[hp * pair:(hp + 1) * pair, :]
            ys, ss = [], []
            for h in (2 * hp, 2 * hp + 1):
                seg = acum[:, h:h + 1] - acum_t[h:h + 1, :]
                wm = jnp.where(causal, jnp.exp(seg), 0.0) * cbm * dt_t[h:h + 1, :]
                ys.append(_dot(wm.astype(BF16), xp_bf))
                bw = (b_g * to_end[:, h:h + 1]).astype(BF16)
                ss.append(_dot(xp_t, bw))
            y_pair = jnp.where(lo_lane, ys[0], ys[1])
            e_pair = jnp.where(lo_lane, eacum[:, 2 * hp:2 * hp + 1], eacum[:, 2 * hp + 1:2 * hp + 2])
            y_pair = y_pair + _dot_nt(c_bf, hpair.astype(BF16)) * e_pair
            s_pair = jnp.where(lo_row, ss[0], ss[1])
            cd_pair = jnp.where(lo_row, cdec_t[2 * hp:2 * hp + 1, :], cdec_t[2 * hp + 1:2 * hp + 2, :])
            h_ref[hp * pair:(hp + 1) * pair, :] = hpair * cd_pair + s_pair
            y_parts.append(y_pair)
        cols = slice(g * grp_w, (g + 1) * grp_w)
        yg = jnp.concatenate(y_parts, axis=1) + dfull_ref[:, cols] * act_ref[:, cols]
        z_g = zx_ref[:, cols]
        yg = yg * (z_g * _sigmoid(z_g))
        ms = jnp.mean(yg * yg, axis=-1, keepdims=True)
        y_ref[:, cols] = (yg * lax.rsqrt(ms + EPS) * nw_ref[:, cols]).astype(y_ref.dtype)


def _ssd(zx, dtr, p, buf0, h0, *, nb, nchunk, tc, t_valid, row_block0):
    d_mix = p["d_mix"]
    wz = zx.shape[1]
    xbc = wz - d_mix
    kw = p["ssd_conv_w"].shape[0]
    hp_rows = h0.shape[1]
    blk = lambda b, c: (row_block0 + b * nchunk + c, 0)
    const = lambda b, c: (0, 0)
    per_b = lambda b, c: (b, 0, 0)
    kern = functools.partial(_ssd_kernel, tc=tc, t_valid=t_valid, d_mix=d_mix, kw=kw)
    return pl.pallas_call(
        kern,
        grid=(nb, nchunk),
        in_specs=[pl.BlockSpec((tc, wz), blk), pl.BlockSpec((tc, LANES), blk),
                  pl.BlockSpec((kw, xbc), const), pl.BlockSpec((1, xbc), const),
                  pl.BlockSpec((1, LANES), const), pl.BlockSpec((1, LANES), const),
                  pl.BlockSpec((1, d_mix), const), pl.BlockSpec((1, d_mix), const),
                  pl.BlockSpec((None, SUBLANES, xbc), per_b), pl.BlockSpec((None, hp_rows, SSD_STATE), per_b)],
        out_specs=[pl.BlockSpec((tc, d_mix), lambda b, c: (b * nchunk + c, 0)),
                   pl.BlockSpec((None, hp_rows, SSD_STATE), per_b)],
        out_shape=[jax.ShapeDtypeStruct((nb * nchunk * tc, d_mix), BF16),
                   jax.ShapeDtypeStruct((nb, hp_rows, SSD_STATE), F32)],
        scratch_shapes=[pltpu.VMEM((tc + SUBLANES, xbc), F32), pltpu.VMEM((tc, xbc), F32)],
        compiler_params=_params(("parallel", "arbitrary")),
    )(zx, dtr, p["ssd_conv_w"], p["ssd_conv_b"], p["ssd_dt_bias"], p["ssd_a_log"], p["ssd_d_full"],
      p["ssd_norm"], buf0, h0)


CONV_HALO = 32


def _conv_kernel(u_ref, buf0_ref, w_ref, b_ref, g_ref, beta_ref, o_ref, ext_ref, acc_ref, *, tt, kw, rb, lb):
    i = pl.program_id(1)
    ch = u_ref.shape[1]

    @pl.when(i == 0)
    def _():
        ext_ref[0:CONV_HALO, :] = buf0_ref[...]

    ext_ref[CONV_HALO:CONV_HALO + tt, :] = u_ref[...]
    base = CONV_HALO - (kw - 1)

    def lane_block(cb, carry):
        cols = pl.ds(pl.multiple_of(cb * lb, lb), lb)
        for rblk in range(tt // rb):
            t0 = rblk * rb
            acc = jnp.broadcast_to(b_ref[:, cols], (rb, lb))
            for r in range(SUBLANES):
                taps = [(a, SUBLANES * a + r - base) for a in range((kw + base) // SUBLANES + 1)
                        if 0 <= SUBLANES * a + r - base < kw]
                win = rb if r == 0 else rb + SUBLANES
                z = None
                for a, k in taps:
                    term = w_ref[k:k + 1, cols] * ext_ref[pl.ds(t0 + SUBLANES * a, win), cols]
                    z = term if z is None else z + term
                acc = acc + (z if r == 0 else pltpu.roll(z, win - r, 0)[:rb])
            acc_ref[pl.ds(t0, rb), cols] = acc
        return carry

    lax.fori_loop(0, ch // lb, lane_block, 0)
    ext_ref[0:CONV_HALO, :] = ext_ref[tt:tt + CONV_HALO, :]

    x = acc_ref[...]
    mu = jnp.mean(x, axis=-1, keepdims=True)
    xc = x - mu
    var = jnp.mean(xc * xc, axis=-1, keepdims=True)
    y = xc * lax.rsqrt(var + EPS) * g_ref[...] + beta_ref[...]
    o_ref[...] = (y * _sigmoid(y)).astype(o_ref.dtype)


def _conv(u, p, buf0, *, nb, ntile, tt, row_block0):
    ch = u.shape[1]
    kw = p["conv_w"].shape[0]
    rb = min(tt, 64)
    lb = LANES
    const = lambda b, i: (0, 0)
    kern = functools.partial(_conv_kernel, tt=tt, kw=kw, rb=rb, lb=lb)
    return pl.pallas_call(
        kern,
        grid=(nb, ntile),
        in_specs=[pl.BlockSpec((tt, ch), lambda b, i: (row_block0 + b * ntile + i, 0)),
                  pl.BlockSpec((None, CONV_HALO, ch), lambda b, i: (b, 0, 0)),
                  pl.BlockSpec((kw, ch), const), pl.BlockSpec((1, ch), const),
                  pl.BlockSpec((1, ch), const), pl.BlockSpec((1, ch), const)],
        out_specs=pl.BlockSpec((tt, ch), lambda b, i: (b * ntile + i, 0)),
        out_shape=jax.ShapeDtypeStruct((nb * ntile * tt, ch), BF16),
        scratch_shapes=[pltpu.VMEM((tt + CONV_HALO, ch), F32), pltpu.VMEM((tt, ch), F32)],
        compiler_params=_params(("parallel", "arbitrary")),
    )(u, buf0, p["conv_w"], p["conv_b"], p["conv_ln_g"], p["conv_ln_b"])


def _moba_prompt_kernel(q_ref, k_ref, v_ref, o_ref, kmean_ref, kb_ref, vt_ref, qt_ref, sel_ref, acc_ref,
                        *, nblk, hpg, scale):
    i = pl.program_id(2)
    blk = MOBA_BLOCK
    hd = ATT_HEAD_DIM
    nq = hpg * blk
    nch = nq // LANES

    @pl.when(i == 0)
    def _():
        kmean_ref[...] = jnp.mean(k_ref[...].reshape(nblk, blk, hd), axis=1)
        for jb in range(nblk):
            kb_ref[jb] = k_ref[jb * blk:(jb + 1) * blk, :].astype(BF16)
            vt_ref[jb] = v_ref[jb * blk:(jb + 1) * blk, :].T.astype(BF16)

    qt = jnp.concatenate([q_ref[:, hh * hd:(hh + 1) * hd].T for hh in range(hpg)], axis=1)
    qt_ref[...] = qt.astype(BF16)
    gate = jnp.dot(kmean_ref[...], qt, preferred_element_type=F32, precision=lax.Precision.HIGHEST)
    blk_row = lax.broadcasted_iota(jnp.int32, (nblk, nq), 0)
    gate = jnp.where(blk_row < i, gate, NEG_INF)
    rank = jnp.zeros((nblk, nq), jnp.int32)
    for jp in range(nblk):
        gj = gate[jp:jp + 1, :]
        rank = rank + jnp.where(gj > gate, 1, jnp.where((gj == gate) & (blk_row > jp), 1, 0))
    sel = jnp.where((rank < MOBA_TOPK) & (gate > NEG_INF), 1.0, 0.0)
    for jb in range(nblk):
        sel_ref[jb] = jnp.broadcast_to(sel[jb:jb + 1, :], (SUBLANES, nq))

    key_i = lax.broadcasted_iota(jnp.int32, (blk, LANES), 0)
    lane_i = lax.broadcasted_iota(jnp.int32, (blk, LANES), 1)

    def block_update(j, m, l, diagonal):
        kj = kb_ref[j]
        vtj = vt_ref[j]
        m_out, l_out = [], []
        for c in range(nch):
            cs = slice(c * LANES, (c + 1) * LANES)
            s = _dot(kj, qt_ref[:, cs]) * scale
            if diagonal:
                q_off = (c * LANES) % blk
                s = jnp.where(key_i <= lane_i + q_off, s, NEG_INF)
                m_new = jnp.max(s, axis=0, keepdims=True)
                p = jnp.exp(s - m_new)
                l_out.append(jnp.sum(p, axis=0, keepdims=True))
                acc_ref[:, cs] = _dot(vtj, p.astype(BF16))
            else:
                s = jnp.where(sel_ref[j, 0:1, cs] > 0.0, s, NEG_INF)
                m_old = m[:, cs]
                m_new = jnp.maximum(m_old, jnp.max(s, axis=0, keepdims=True))
                alpha = jnp.exp(m_old - m_new)
                p = jnp.exp(s - m_new)
                l_out.append(alpha * l[:, cs] + jnp.sum(p, axis=0, keepdims=True))
                acc_ref[:, cs] = alpha * acc_ref[:, cs] + _dot(vtj, p.astype(BF16))
            m_out.append(m_new)
        return jnp.concatenate(m_out, axis=1), jnp.concatenate(l_out, axis=1)

    m0, l0 = block_update(i, None, None, True)
    m, l = lax.fori_loop(0, i, lambda j, ml: block_update(j, ml[0], ml[1], False), (m0, l0))
    out_t = acc_ref[...] / l
    for hh in range(hpg):
        o_ref[:, hh * hd:(hh + 1) * hd] = out_t[:, hh * blk:(hh + 1) * blk].T.astype(o_ref.dtype)


def _moba_prompt(qkv, *, nb, t_len, n_heads, kv_heads):
    hd = ATT_HEAD_DIM
    blk = MOBA_BLOCK
    assert t_len % blk == 0
    nblk = t_len // blk
    hpg = n_heads // kv_heads
    kcol0 = n_heads
    vcol0 = n_heads + kv_heads
    kern = functools.partial(_moba_prompt_kernel, nblk=nblk, hpg=hpg, scale=hd ** -0.5)
    return pl.pallas_call(
        kern,
        grid=(nb, kv_heads, nblk),
        in_specs=[pl.BlockSpec((blk, hpg * hd), lambda b, g, i: (b * nblk + i, g)),
                  pl.BlockSpec((t_len, hd), lambda b, g, i: (b, kcol0 + g)),
                  pl.BlockSpec((t_len, hd), lambda b, g, i: (b, vcol0 + g))],
        out_specs=pl.BlockSpec((blk, hpg * hd), lambda b, g, i: (b * nblk + i, g)),
        out_shape=jax.ShapeDtypeStruct((nb * t_len, n_heads * hd), BF16),
        scratch_shapes=[pltpu.VMEM((nblk, hd), F32), pltpu.VMEM((nblk, blk, hd), BF16),
                        pltpu.VMEM((nblk, hd, blk), BF16), pltpu.VMEM((hd, hpg * blk), BF16),
                        pltpu.VMEM((nblk, SUBLANES, hpg * blk), F32), pltpu.VMEM((hd, hpg * blk), F32)],
        compiler_params=_params(("parallel", "parallel", "arbitrary")),
    )(qkv, qkv, qkv)


def _moba_sample_kernel(pt_ref, q_ref, *refs, nblk, bps, ppb, kv_heads, rows_per_group, dt_len, scale):
    del pt_ref
    npg = bps * ppb
    kpages, vpages = refs[:npg], refs[npg:2 * npg]
    knew_ref, vnew_ref, o_ref, gate_ref, m_ref, l_ref, acc_ref = refs[2 * npg:]
    step = pl.program_id(1)
    hd = ATT_HEAD_DIM
    n_rows = q_ref.shape[0]
    rpg = rows_per_group
    n_keys = MOBA_BLOCK * kv_heads
    q = q_ref[...]
    lane = lax.broadcasted_iota(jnp.int32, (n_rows, LANES), 1)

    @pl.when(step == 0)
    def _():
        gate_ref[...] = jnp.full((n_rows, LANES), NEG_INF, F32)
        m_ref[...] = jnp.zeros((n_rows, LANES), F32)
        l_ref[...] = jnp.zeros((n_rows, LANES), F32)

    key_head = lax.broadcasted_iota(jnp.int32, (n_rows, n_keys), 1) % kv_heads
    row_head = lax.broadcasted_iota(jnp.int32, (n_rows, n_keys), 0) // rpg
    own_head = key_head == row_head
    g_lane = lax.broadcasted_iota(jnp.int32, (n_rows, SUBLANES), 1) % kv_heads
    g_rowh = lax.broadcasted_iota(jnp.int32, (n_rows, SUBLANES), 0) // rpg
    q_bf = q.astype(BF16)
    gate_new, m_new, l_new = gate_ref[...], m_ref[...], l_ref[...]
    for i in range(bps):
        j = step * bps + i
        kblk = jnp.concatenate([r[...] for r in kpages[i * ppb:(i + 1) * ppb]], axis=0)
        vblk = jnp.concatenate([r[...] for r in vpages[i * ppb:(i + 1) * ppb]], axis=0)
        s = _dot_nt(q_bf, kblk.astype(BF16)) * scale
        s = jnp.where(own_head, s, NEG_INF)
        m_col = jnp.max(s, axis=-1, keepdims=True)
        p = jnp.exp(s - m_col)
        l_col = jnp.sum(p, axis=-1, keepdims=True)
        acc_ref[j] = _dot(p.astype(BF16), vblk.astype(BF16))
        ksub = jnp.sum(kblk.reshape(n_keys // SUBLANES, SUBLANES, hd), axis=0) * (1.0 / MOBA_BLOCK)
        gall = _dot_nt(q, ksub, precision=lax.Precision.HIGHEST)
        g_col = jnp.sum(jnp.where(g_lane == g_rowh, gall, 0.0), axis=-1, keepdims=True)
        is_j = lane == j
        gate_new = jnp.where(is_j, g_col, gate_new)
        m_new = jnp.where(is_j, m_col, m_new)
        l_new = jnp.where(is_j, l_col, l_new)
    gate_ref[...] = gate_new
    m_ref[...] = m_new
    l_ref[...] = l_new

    def group_partials(g, kg, vg, mask):
        qg = q_ref[g * rpg:(g + 1) * rpg, :]
        sg = jnp.where(mask, _dot_nt(qg.astype(BF16), kg.astype(BF16)) * scale, NEG_INF)
        mg = jnp.max(sg, axis=-1, keepdims=True)
        pg = jnp.exp(sg - mg)
        return mg, jnp.sum(pg, axis=-1, keepdims=True), _dot(pg.astype(BF16), vg.astype(BF16))

    @pl.when(step == nblk // bps - 1)
    def _():
        work = gate_ref[...]
        sel = jnp.zeros((n_rows, LANES), F32)
        for _ in range(min(MOBA_TOPK, nblk)):
            mx = jnp.max(work, axis=-1, keepdims=True)
            idx = jnp.min(jnp.where(work == mx, lane, LANES), axis=-1, keepdims=True)
            pick = lane == idx
            sel = jnp.where(pick & (mx > NEG_INF), 1.0, sel)
            work = jnp.where(pick, NEG_INF, work)
        row_t = lax.broadcasted_iota(jnp.int32, (rpg, knew_ref.shape[0]), 0) % dt_len
        key_i = lax.broadcasted_iota(jnp.int32, (rpg, knew_ref.shape[0]), 1)
        own = [group_partials(g, knew_ref[:, g * hd:(g + 1) * hd], vnew_ref[:, g * hd:(g + 1) * hd],
                              key_i <= row_t) for g in range(kv_heads)]
        m_own = jnp.concatenate([o[0] for o in own], axis=0)
        l_own = jnp.concatenate([o[1] for o in own], axis=0)
        acc_own = jnp.concatenate([o[2] for o in own], axis=0)
        m_all = m_ref[...]
        picked = sel > 0.0
        m_tot = jnp.maximum(jnp.max(jnp.where(picked, m_all, NEG_INF), axis=-1, keepdims=True), m_own)
        w = jnp.where(picked, jnp.exp(m_all - m_tot), 0.0)
        w_own = jnp.exp(m_own - m_tot)
        l_tot = jnp.sum(w * l_ref[...], axis=-1, keepdims=True) + w_own * l_own
        out = w_own * acc_own
        for jj in range(nblk):
            out = out + w[:, jj:jj + 1] * acc_ref[jj]
        o_ref[...] = out / l_tot


def _moba_sample(q_rows, knew, vnew, cache_k, cache_v, page_table, layer, *, kv_heads, dt_len):
    db, n_rows, hd = q_rows.shape
    page_rows = cache_k.shape[2]
    page = page_rows // kv_heads
    width = kv_heads * hd
    ppb = MOBA_BLOCK // page
    n_pages = page_table.shape[1]
    assert n_pages % ppb == 0 and MOBA_BLOCK % page == 0 and SUBLANES % kv_heads == 0
    nblk = n_pages // ppb
    assert nblk <= LANES
    rows_per_group = n_rows // kv_heads

    bps = _pick_tile(nblk, 4, 1)
    pps = bps * ppb

    def page_spec(o):
        return pl.BlockSpec((None, None, page_rows, hd), lambda b, j, pt: (layer, pt[b, pps * j + o], 0, 0))

    page_specs = [page_spec(o) for o in range(pps)]
    per_b = lambda b, j, pt: (b, 0, 0)
    kern = functools.partial(_moba_sample_kernel, nblk=nblk, bps=bps, ppb=ppb, kv_heads=kv_heads,
                             rows_per_group=rows_per_group, dt_len=dt_len, scale=hd ** -0.5)
    grid_spec = pltpu.PrefetchScalarGridSpec(
        num_scalar_prefetch=1,
        grid=(db, nblk // bps),
        in_specs=[pl.BlockSpec((None, n_rows, hd), per_b)]
                 + page_specs + page_specs
                 + [pl.BlockSpec((None, knew.shape[1], width), per_b)] * 2,
        out_specs=pl.BlockSpec((None, n_rows, hd), per_b),
        scratch_shapes=[pltpu.VMEM((n_rows, LANES), F32)] * 3 + [pltpu.VMEM((nblk, n_rows, hd), F32)],
    )
    return pl.pallas_call(
        kern,
        grid_spec=grid_spec,
        out_shape=jax.ShapeDtypeStruct((db, n_rows, hd), F32),
        compiler_params=_params(("parallel", "arbitrary")),
    )(page_table, q_rows, *([cache_k] * len(page_specs)), *([cache_v] * len(page_specs)), knew, vnew)


def _rope_tables(pos):
    half = ATT_HEAD_DIM // 2
    inv_freq = jnp.exp(-math.log(ROPE_THETA) * jnp.arange(half, dtype=F32) * 2.0 / ATT_HEAD_DIM)
    ang = pos[:, None] * inv_freq[None, :]
    cos, sin = jnp.cos(ang), jnp.sin(ang)
    return jnp.concatenate([cos, cos], axis=-1), jnp.concatenate([-sin, sin], axis=-1)


def _prep_weights(d_model, w_in, w_ssd_out, w_conv_out, w_att_out, w_o, w_up, w_down, conv_ch, n_heads, kv_heads):
    d_mix = d_model // 2
    ssd_heads = d_mix // SSD_HEADDIM
    xbc = d_mix + 2 * SSD_GROUPS * SSD_STATE
    o_dt = d_mix + xbc
    o_conv = o_dt + ssd_heads
    qkv_w = (n_heads + 2 * kv_heads) * ATT_HEAD_DIM
    return {
        "w_in": w_in, "o_dt": o_dt,
        "w_rest": _cast_tail(w_in, o_conv),
        "o_val": 0, "o_gate": conv_ch, "o_qkv": 2 * conv_ch, "o_g": 2 * conv_ch + qkv_w,
        "w_branch": jnp.stack([w_ssd_out, w_conv_out, w_att_out]).astype(BF16),
        "w_o": w_o, "w_up": w_up, "w_down": w_down,
    }


def _prep_layer(l, d_mix, ssd_conv_w, ssd_conv_b, ssd_dt_bias, ssd_a_log, ssd_d, ssd_norm,
                conv_w, conv_b, conv_ln_g, conv_ln_b, norm_mix, norm_ffn):
    xbc = ssd_conv_w.shape[2]
    conv_ch = conv_w.shape[2]
    pad_row = lambda v: jnp.pad(v, (0, LANES - v.shape[0])).reshape(1, LANES)
    return {
        "d_mix": d_mix,
        "norm_mix": norm_mix[l], "norm_ffn": norm_ffn[l],
        "ssd_conv_w": ssd_conv_w[l], "ssd_conv_b": ssd_conv_b[l].reshape(1, xbc),
        "ssd_dt_bias": pad_row(ssd_dt_bias[l]), "ssd_a_log": pad_row(ssd_a_log[l]),
        "ssd_d_full": jnp.repeat(ssd_d[l], SSD_HEADDIM).reshape(1, d_mix),
        "ssd_norm": ssd_norm[l].reshape(1, d_mix),
        "conv_w": conv_w[l], "conv_b": conv_b[l].reshape(1, conv_ch),
        "conv_ln_g": conv_ln_g[l].reshape(1, conv_ch), "conv_ln_b": conv_ln_b[l].reshape(1, conv_ch),
    }


def _last_rows(a, nb, t_len, n, col0=0):
    return jnp.stack([a[(b + 1) * t_len - n:(b + 1) * t_len, col0:] for b in range(nb)])


def _layer(x, p, w, l, cos_t, sin_t, cache_k, cache_v, page_table, state_ssm_l, state_ssm_conv_l, state_conv_l,
           *, nbp, t_len, dbs, dt_len, n_heads, kv_heads):
    m, d_model = x.shape
    d_mix = p["d_mix"]
    hd = ATT_HEAD_DIM
    bt = nbp * t_len
    xbc_w = p["ssd_conv_w"].shape[1]
    conv_ch = p["conv_w"].shape[1]
    kv_w = kv_heads * hd

    h = _rmsnorm(x, p["norm_mix"], BF16)
    zx = _matmul(h, w["w_in"], l, n=w["o_dt"])
    dtr = _matmul(h, w["w_in"], l, tn=LANES, col0=w["o_dt"], n=LANES)
    u = _matmul_glu(h, w["w_rest"], l, w["o_val"], w["o_gate"], conv_ch)
    n_rope_tiles = (n_heads + kv_heads) * hd // 512
    qkv = _matmul(h, w["w_rest"], l, epilogue=functools.partial(_ep_rope, n_rope_tiles=n_rope_tiles),
                  extra=(cos_t, sin_t), extra_specs=(_spec_row_table, _spec_row_table), split_rows=True,
                  col0=w["o_qkv"], n=w["o_g"] - w["o_qkv"])
    gates = _matmul(h, w["w_rest"], l, epilogue=_ep_sigmoid, out_dtype=BF16, split_rows=True, col0=w["o_g"])

    tc = 128
    ncp = t_len // tc
    y_p, ssm_p = _ssd(zx, dtr, p, jnp.zeros((nbp, SUBLANES, xbc_w), F32),
                      jnp.zeros((nbp, d_mix, SSD_STATE), F32),
                      nb=nbp, nchunk=ncp, tc=tc, t_valid=tc, row_block0=0)
    zx_s = zx[bt:].reshape(dbs, dt_len, -1)
    pad_t = ((0, 0), (0, tc - dt_len), (0, 0))
    zx_sp = jnp.pad(zx_s, pad_t).reshape(dbs * tc, -1)
    dtr_sp = jnp.pad(dtr[bt:].reshape(dbs, dt_len, LANES), pad_t).reshape(dbs * tc, LANES)
    kc = state_ssm_conv_l.shape[1]
    buf0_s = jnp.pad(state_ssm_conv_l, ((0, 0), (SUBLANES - kc, 0), (0, 0)))
    y_s, ssm_s = _ssd(zx_sp, dtr_sp, p, buf0_s, state_ssm_l.reshape(dbs, d_mix, SSD_STATE),
                      nb=dbs, nchunk=1, tc=tc, t_valid=dt_len, row_block0=0)
    y_s = y_s.reshape(dbs, tc, d_mix)[:, :dt_len].reshape(dbs * dt_len, d_mix)
    ssm_buf_p = _last_rows(zx, nbp, t_len, kc, d_mix)
    ssm_buf_s = jnp.concatenate([state_ssm_conv_l, zx_s[:, :, d_mix:]], axis=1)[:, -kc:]

    tt = 256
    cw1 = p["conv_w"].shape[0] - 1
    c_p = _conv(u, p, jnp.zeros((nbp, CONV_HALO, conv_ch), F32), nb=nbp, ntile=t_len // tt, tt=tt, row_block0=0)
    buf0_c = jnp.pad(state_conv_l, ((0, 0), (CONV_HALO - cw1, 0), (0, 0)))
    c_s = _conv(u, p, buf0_c, nb=dbs, ntile=1, tt=dt_len, row_block0=bt // dt_len)
    conv_buf_p = _last_rows(u, nbp, t_len, cw1)
    conv_buf_s = jnp.concatenate([state_conv_l, u[bt:].reshape(dbs, dt_len, conv_ch)], axis=1)[:, -cw1:]

    a_p = _moba_prompt(qkv, nb=nbp, t_len=t_len, n_heads=n_heads, kv_heads=kv_heads)
    qkv_s = qkv[bt:]
    q_rows = qkv_s[:, :n_heads * hd].reshape(dbs, dt_len, n_heads, hd).transpose(0, 2, 1, 3)
    q_rows = q_rows.reshape(dbs, n_heads * dt_len, hd)
    k_s = qkv_s[:, n_heads * hd:n_heads * hd + kv_w].reshape(dbs, dt_len, kv_w)
    v_s = qkv_s[:, n_heads * hd + kv_w:].reshape(dbs, dt_len, kv_w)
    pad_k = ((0, 0), (0, LANES - dt_len), (0, 0))
    a_s = _moba_sample(q_rows, jnp.pad(k_s, pad_k), jnp.pad(v_s, pad_k), cache_k, cache_v, page_table, l,
                       kv_heads=kv_heads, dt_len=dt_len)
    a_s = a_s.reshape(dbs, n_heads, dt_len, hd).transpose(0, 2, 1, 3).reshape(dbs * dt_len, n_heads * hd)
    k_p = qkv[:bt, n_heads * hd:n_heads * hd + kv_w].reshape(nbp, t_len, kv_heads, hd)
    v_p = qkv[:bt, n_heads * hd + kv_w:].reshape(nbp, t_len, kv_heads, hd)

    y3 = jnp.stack([jnp.concatenate([y_p, y_s]), jnp.concatenate([c_p, c_s]),
                    jnp.concatenate([a_p, a_s.astype(BF16)])])
    merged = _merge(y3, w["w_branch"], gates, l)
    res_spec = (_spec_out_tile,)
    x = _matmul(merged, w["w_o"], l, epilogue=_ep_residual, extra=(x,), extra_specs=res_spec)
    hf = _rmsnorm(x, p["norm_ffn"], BF16)
    act = _matmul(hf, w["w_up"], l, epilogue=_ep_relu2, out_dtype=BF16)
    x = _matmul(act, w["w_down"], l, epilogue=_ep_residual, extra=(x,), extra_specs=res_spec, tk=4096)
    states = (k_p, v_p, k_s.reshape(dbs, dt_len, kv_heads, hd), v_s.reshape(dbs, dt_len, kv_heads, hd),
              ssm_p.reshape(nbp, -1, SSD_HEADDIM, SSD_STATE), ssm_s.reshape(dbs, -1, SSD_HEADDIM, SSD_STATE),
              ssm_buf_p, ssm_buf_s, conv_buf_p, conv_buf_s)
    return x, states


def kernel(x_prompt, x_sample, cache_k, cache_v, page_table, state_ssm, state_ssm_conv, state_conv, norm_mix, w_in, ssd_conv_w, ssd_conv_b, ssd_dt_bias, ssd_a_log, ssd_d, ssd_norm, w_ssd_out, conv_w, conv_b, conv_ln_g, conv_ln_b, w_conv_out, w_att_out, w_o, norm_ffn, w_up, w_down, norm_final):
    nbp, t_len, d_model = x_prompt.shape
    dbs, dt_len, _ = x_sample.shape
    depth = w_in.shape[0]
    n_phys, page, kv_heads, hd = cache_k.shape[1:]
    assert hd == ATT_HEAD_DIM
    n_heads = w_att_out.shape[1] // hd
    past_len = page_table.shape[1] * page
    assert past_len % MOBA_BLOCK == 0 and dt_len <= SUBLANES
    bt = nbp * t_len

    x = jnp.concatenate([x_prompt.reshape(bt, d_model), x_sample.reshape(dbs * dt_len, d_model)], axis=0)
    pos = jnp.concatenate([jnp.tile(jnp.arange(t_len, dtype=F32), nbp),
                           jnp.tile(past_len + jnp.arange(dt_len, dtype=F32), dbs)])
    cos_t, sin_t = _rope_tables(pos)
    ck = cache_k.reshape(depth, n_phys, page * kv_heads, hd)
    cv = cache_v.reshape(depth, n_phys, page * kv_heads, hd)
    w = _prep_weights(d_model, w_in, w_ssd_out, w_conv_out, w_att_out, w_o, w_up, w_down, conv_w.shape[2],
                      n_heads, kv_heads)

    per_layer = []
    for l in range(depth):
        p = _prep_layer(l, d_model // 2, ssd_conv_w, ssd_conv_b, ssd_dt_bias, ssd_a_log, ssd_d, ssd_norm,
                        conv_w, conv_b, conv_ln_g, conv_ln_b, norm_mix, norm_ffn)
        x, st = _layer(x, p, w, l, cos_t, sin_t, ck, cv, page_table, state_ssm[l], state_ssm_conv[l],
                       state_conv[l], nbp=nbp, t_len=t_len, dbs=dbs, dt_len=dt_len, n_heads=n_heads,
                       kv_heads=kv_heads)
        per_layer.append(st)
    y_prompt = _rmsnorm(x, norm_final, F32, 0, bt).reshape(nbp, t_len, d_model)
    y_sample = _rmsnorm(x, norm_final, F32, bt, dbs * dt_len).reshape(dbs, dt_len, d_model)
    stacked = [jnp.stack([st[i] for st in per_layer]) for i in range(10)]
    return (y_prompt, y_sample, *stacked)
```

```python
import functools
import math

import jax
import jax.numpy as jnp
from jax import lax
from jax.experimental import pallas as pl
from jax.experimental.pallas import tpu as pltpu

F32 = jnp.float32
BF16 = jnp.bfloat16

SSD_HEADDIM = 64
SSD_STATE = 128
SSD_GROUPS = 4
ATT_HEAD_DIM = 128
MOBA_BLOCK = 256
MOBA_TOPK = 3
ROPE_THETA = 10000.0
EPS = 1e-6

LANES = 128
SUBLANES = 8
BF16_ROWS = 16
VMEM_LIMIT = 56 * 1024 * 1024

NEG_INF = float("-inf")


def _pick_tile(n, cap, mult):
    best = None
    for d in range(mult, min(n, cap) + 1, mult):
        if n % d == 0:
            best = d
    return n if best is None else best


def _params(sem):
    return pltpu.CompilerParams(dimension_semantics=sem, vmem_limit_bytes=VMEM_LIMIT)


def _sigmoid(x):
    return jax.nn.sigmoid(x)


def _dot(a, b):
    return jnp.dot(a, b, preferred_element_type=F32)


def _dot_nt(a, b, precision=None):
    return lax.dot_general(a, b, (((1,), (1,)), ((), ())), preferred_element_type=F32,
                           precision=precision)


def _rmsnorm_kernel(x_ref, g_ref, o_ref):
    x = x_ref[...]
    ms = jnp.mean(x * x, axis=-1, keepdims=True)
    o_ref[...] = (x * lax.rsqrt(ms + EPS) * g_ref[...]).astype(o_ref.dtype)


def _rmsnorm(x, g, out_dtype, row0=0, rows=None):
    d = x.shape[1]
    rows = x.shape[0] if rows is None else rows
    tr = _pick_tile(math.gcd(rows, row0) if row0 else rows, 256, 16)
    blk0 = row0 // tr
    return pl.pallas_call(
        _rmsnorm_kernel,
        grid=(rows // tr,),
        in_specs=[pl.BlockSpec((tr, d), lambda i: (blk0 + i, 0)), pl.BlockSpec((1, d), lambda i: (0, 0))],
        out_specs=pl.BlockSpec((tr, d), lambda i: (i, 0)),
        out_shape=jax.ShapeDtypeStruct((rows, d), out_dtype),
        compiler_params=_params(("parallel",)),
    )(x, g.reshape(1, d))


def _ep_store(acc, o_ref):
    o_ref[...] = acc.astype(o_ref.dtype)


def _ep_sigmoid(acc, o_ref):
    o_ref[...] = _sigmoid(acc).astype(o_ref.dtype)


def _ep_relu2(acc, o_ref):
    r = jnp.maximum(acc, 0.0)
    o_ref[...] = (r * r).astype(o_ref.dtype)


def _ep_residual(acc, o_ref, res_ref):
    o_ref[...] = (res_ref[...] + acc).astype(o_ref.dtype)


def _ep_rope(acc, o_ref, cos_ref, sin_ref, *, n_rope_tiles):
    j = pl.program_id(1)
    c = cos_ref[...]
    s = sin_ref[...]
    do_rope = j < n_rope_tiles
    for h in range(acc.shape[1] // ATT_HEAD_DIM):
        a = acc[:, h * ATT_HEAD_DIM:(h + 1) * ATT_HEAD_DIM]
        partner = pltpu.roll(a, ATT_HEAD_DIM // 2, 1)
        o_ref[:, h * ATT_HEAD_DIM:(h + 1) * ATT_HEAD_DIM] = jnp.where(do_rope, a * c + partner * s, a)


def _row_chunks(tm):
    if tm % (2 * BF16_ROWS):
        return [(0, tm)]
    return [(0, tm // 2), (tm // 2, tm // 2)]


CAST_K = 1024


def _x_dot_w(x_ref, rows, w_ref):
    if w_ref.dtype == BF16:
        return _dot(x_ref[rows, :], w_ref[...])
    acc = None
    for k0 in range(0, w_ref.shape[0], CAST_K):
        part = _dot(x_ref[rows, k0:k0 + CAST_K], w_ref[k0:k0 + CAST_K, :].astype(BF16))
        acc = part if acc is None else acc + part
    return acc


def _mm_kernel(*refs, nk, n_extra, epilogue, split_rows):
    x_ref, w_ref = refs[0], refs[1]
    extra = refs[2:2 + n_extra]
    o_ref = refs[2 + n_extra]
    tm = x_ref.shape[0]
    if nk == 1:
        for r0, rc in (_row_chunks(tm) if split_rows else [(0, tm)]):
            rows = pl.ds(r0, rc)
            epilogue(_x_dot_w(x_ref, rows, w_ref), o_ref.at[rows], *[e.at[rows] for e in extra])
        return
    acc_ref = refs[3 + n_extra]
    k = pl.program_id(2)
    all_rows = pl.ds(0, tm)

    @pl.when(k == 0)
    def _():
        acc_ref[...] = _x_dot_w(x_ref, all_rows, w_ref)

    @pl.when(k > 0)
    def _():
        acc_ref[...] += _x_dot_w(x_ref, all_rows, w_ref)

    @pl.when(k == nk - 1)
    def _():
        epilogue(acc_ref[...], o_ref, *extra)


def _matmul(x, w, layer, *, epilogue=_ep_store, out_dtype=F32, tm_cap=1376, tn=512, tk=None,
            extra=(), extra_specs=(), split_rows=False, col0=0, n=None):
    m, kdim = x.shape
    n = w.shape[2] - col0 if n is None else n
    tm = _pick_tile(m, tm_cap, 16)
    tn = min(tn, n)
    tk = kdim if tk is None else tk
    assert n % tn == 0 and kdim % tk == 0 and col0 % tn == 0 and col0 + n <= w.shape[2]
    cb0 = col0 // tn
    nk = kdim // tk
    grid = (m // tm, n // tn, nk)
    kern = functools.partial(_mm_kernel, nk=nk, n_extra=len(extra), epilogue=epilogue, split_rows=split_rows)
    scratch = [] if nk == 1 else [pltpu.VMEM((tm, tn), F32)]
    return pl.pallas_call(
        kern,
        grid=grid,
        in_specs=[pl.BlockSpec((tm, tk), lambda i, j, k: (i, k)),
                  pl.BlockSpec((None, tk, tn), lambda i, j, k: (layer, k, cb0 + j))]
                 + [s(tm, tn) for s in extra_specs],
        out_specs=pl.BlockSpec((tm, tn), lambda i, j, k: (i, j)),
        out_shape=jax.ShapeDtypeStruct((m, n), out_dtype),
        scratch_shapes=scratch,
        compiler_params=_params(("parallel", "parallel", "arbitrary")),
    )(x, w, *extra)


def _glu_kernel(x_ref, wv_ref, wg_ref, o_ref):
    x = x_ref[...]
    o_ref[...] = (_dot(x, wv_ref[...]) * _sigmoid(_dot(x, wg_ref[...]))).astype(o_ref.dtype)


def _matmul_glu(x, w, layer, col_val, col_gate, n, *, tm_cap=1376, tn=256):
    m, kdim = x.shape
    tm = _pick_tile(m, tm_cap, 16)
    assert n % tn == 0 and col_val % tn == 0 and col_gate % tn == 0
    bv, bg = col_val // tn, col_gate // tn
    return pl.pallas_call(
        _glu_kernel,
        grid=(m // tm, n // tn),
        in_specs=[pl.BlockSpec((tm, kdim), lambda i, j: (i, 0)),
                  pl.BlockSpec((None, kdim, tn), lambda i, j: (layer, 0, bv + j)),
                  pl.BlockSpec((None, kdim, tn), lambda i, j: (layer, 0, bg + j))],
        out_specs=pl.BlockSpec((tm, tn), lambda i, j: (i, j)),
        out_shape=jax.ShapeDtypeStruct((m, n), F32),
        compiler_params=_params(("parallel", "parallel")),
    )(x, w, w)


def _spec_row_table(tm, tn):
    return pl.BlockSpec((tm, ATT_HEAD_DIM), lambda i, j, k: (i, 0))


def _spec_out_tile(tm, tn):
    return pl.BlockSpec((tm, tn), lambda i, j, k: (i, j))


def _transpose_cast_kernel(wt_ref, o_ref):
    o_ref[...] = wt_ref[...].T.astype(o_ref.dtype)


def _transpose_cast(wt, row0, n, tn):
    nl, _, kdim = wt.shape
    assert n % tn == 0 and row0 % SUBLANES == 0
    return pl.pallas_call(
        _transpose_cast_kernel,
        grid=(nl, n // tn),
        in_specs=[pl.BlockSpec((pl.Squeezed(), pl.Element(tn), pl.Element(kdim)),
                               lambda l, j: (l, pl.multiple_of(row0 + j * tn, SUBLANES), 0))],
        out_specs=pl.BlockSpec((None, kdim, tn), lambda l, j: (l, 0, j)),
        out_shape=jax.ShapeDtypeStruct((nl, kdim, n), BF16),
        compiler_params=_params(("parallel", "parallel")),
    )(wt)


def _merge_kernel(y_ref, w_ref, g_ref, o_ref, tot_ref):
    b = pl.program_id(2)
    chunks = _row_chunks(y_ref.shape[0])

    def gated(rows):
        return g_ref[rows, :].astype(F32) * _dot(y_ref[rows, :], w_ref[...])

    @pl.when(b == 0)
    def _():
        for r0, rc in chunks:
            rows = pl.ds(r0, rc)
            tot_ref[rows, :] = gated(rows)

    @pl.when(b > 0)
    def _():
        for r0, rc in chunks:
            rows = pl.ds(r0, rc)
            tot = tot_ref[rows, :] + gated(rows)
            tot_ref[rows, :] = tot
            o_ref[rows, :] = tot.astype(o_ref.dtype)


def _merge(y3, w3, g, layer, tm_cap=1376, tn=1024):
    nb, m, kb = y3.shape
    d = w3.shape[3]
    tm = _pick_tile(m, tm_cap, 16)
    ncol = d // tn
    return pl.pallas_call(
        _merge_kernel,
        grid=(m // tm, ncol, nb),
        in_specs=[pl.BlockSpec((None, tm, kb), lambda i, j, b: (b, i, 0)),
                  pl.BlockSpec((None, None, kb, tn), lambda i, j, b: (b, layer, 0, j)),
                  pl.BlockSpec((tm, tn), lambda i, j, b: (i, b * ncol + j))],
        out_specs=pl.BlockSpec((tm, tn), lambda i, j, b: (i, j)),
        out_shape=jax.ShapeDtypeStruct((m, d), BF16),
        scratch_shapes=[pltpu.VMEM((tm, tn), F32)],
        compiler_params=_params(("parallel", "parallel", "arbitrary")),
    )(y3, w3, g)


def _ssd_kernel(zx_ref, dt_ref, cw_ref, cb_ref, dtb_ref, alog_ref, dfull_ref, nw_ref, buf0_ref, h0_ref,
                y_ref, h_ref, ext_ref, act_ref, *, tc, t_valid, d_mix, kw):
    c = pl.program_id(1)
    n_state = SSD_STATE
    pair = 2 * SSD_HEADDIM
    grp_w = d_mix // SSD_GROUPS
    pairs_per_group = grp_w // pair

    @pl.when(c == 0)
    def _():
        h_ref[...] = h0_ref[...]
        ext_ref[0:SUBLANES, :] = buf0_ref[...]

    ext_ref[SUBLANES:SUBLANES + tc, :] = zx_ref[:, d_mix:]
    conv = cb_ref[...] + cw_ref[0:1, :] * ext_ref[SUBLANES - (kw - 1):SUBLANES - (kw - 1) + tc, :]
    for k in range(1, kw):
        off = SUBLANES - (kw - 1) + k
        conv = conv + cw_ref[k:k + 1, :] * ext_ref[off:off + tc, :]
    ext_ref[0:SUBLANES, :] = ext_ref[tc:tc + SUBLANES, :]
    act_ref[...] = conv * _sigmoid(conv)

    dt = jax.nn.softplus(dt_ref[...] + dtb_ref[...])
    row = lax.broadcasted_iota(jnp.int32, (tc, LANES), 0)
    if t_valid < tc:
        dt = jnp.where(row < t_valid, dt, 0.0)
    a = -jnp.exp(alog_ref[...])
    da = dt * a
    r_i = lax.broadcasted_iota(jnp.int32, (tc, tc), 0)
    c_i = lax.broadcasted_iota(jnp.int32, (tc, tc), 1)
    causal = c_i <= r_i
    tri = jnp.where(causal, 1.0, 0.0).astype(BF16)
    da_hi = da.astype(BF16)
    rem = da - da_hi.astype(F32)
    da_mid = rem.astype(BF16)
    da_lo = (rem - da_mid.astype(F32)).astype(BF16)
    acum = _dot(tri, da_hi) + _dot(tri, da_mid) + _dot(tri, da_lo)
    acum_t = acum.T
    dt_t = dt.T
    a_tot = acum[tc - 1:tc, :]
    to_end = jnp.exp(a_tot - acum) * dt
    eacum = jnp.exp(acum)
    cdec_t = jnp.exp(acum_t[:, tc - 1:tc])

    lane = lax.broadcasted_iota(jnp.int32, (tc, pair), 1)
    lo_lane = lane < SSD_HEADDIM
    prow = lax.broadcasted_iota(jnp.int32, (pair, n_state), 0)
    lo_row = prow < SSD_HEADDIM

    for g in range(SSD_GROUPS):
        b_g = act_ref[:, d_mix + g * n_state:d_mix + (g + 1) * n_state]
        c_g = act_ref[:, d_mix + (SSD_GROUPS + g) * n_state:d_mix + (SSD_GROUPS + g + 1) * n_state]
        c_bf = c_g.astype(BF16)
        cbm = _dot_nt(c_bf, b_g.astype(BF16))
        y_parts = []
        for pr in range(pairs_per_group):
            hp = g * pairs_per_group + pr
            xp = act_ref[:, hp * pair:(hp + 1) * pair]
            xp_bf = xp.astype(BF16)
            xp_t = xp.T.astype(BF16)
            hpair = h_ref[hp * pair:(hp + 1) * pair, :]
            ys, ss = [], []
            for h in (2 * hp, 2 * hp + 1):
                seg = acum[:, h:h + 1] - acum_t[h:h + 1, :]
                wm = jnp.where(causal, jnp.exp(seg), 0.0) * cbm * dt_t[h:h + 1, :]
                ys.append(_dot(wm.astype(BF16), xp_bf))
                bw = (b_g * to_end[:, h:h + 1]).astype(BF16)
                ss.append(_dot(xp_t, bw))
            y_pair = jnp.where(lo_lane, ys[0], ys[1])
            e_pair = jnp.where(lo_lane, eacum[:, 2 * hp:2 * hp + 1], eacum[:, 2 * hp + 1:2 * hp + 2])
            y_pair = y_pair + _dot_nt(c_bf, hpair.astype(BF16)) * e_pair
            s_pair = jnp.where(lo_row, ss[0], ss[1])
            cd_pair = jnp.where(lo_row, cdec_t[2 * hp:2 * hp + 1, :], cdec_t[2 * hp + 1:2 * hp + 2, :])
            h_ref[hp * pair:(hp + 1) * pair, :] = hpair * cd_pair + s_pair
            y_parts.append(y_pair)
        cols = slice(g * grp_w, (g + 1) * grp_w)
        yg = jnp.concatenate(y_parts, axis=1) + dfull_ref[:, cols] * act_ref[:, cols]
        z_g = zx_ref[:, cols]
        yg = yg * (z_g * _sigmoid(z_g))
        ms = jnp.mean(yg * yg, axis=-1, keepdims=True)
        y_ref[:, cols] = (yg * lax.rsqrt(ms + EPS) * nw_ref[:, cols]).astype(y_ref.dtype)


def _ssd(zx, dtr, p, buf0, h0, *, nb, nchunk, tc, t_valid, row_block0):
    d_mix = p["d_mix"]
    wz = zx.shape[1]
    xbc = wz - d_mix
    kw = p["ssd_conv_w"].shape[0]
    hp_rows = h0.shape[1]
    blk = lambda b, c: (row_block0 + b * nchunk + c, 0)
    const = lambda b, c: (0, 0)
    per_b = lambda b, c: (b, 0, 0)
    kern = functools.partial(_ssd_kernel, tc=tc, t_valid=t_valid, d_mix=d_mix, kw=kw)
    return pl.pallas_call(
        kern,
        grid=(nb, nchunk),
        in_specs=[pl.BlockSpec((tc, wz), blk), pl.BlockSpec((tc, LANES), blk),
                  pl.BlockSpec((kw, xbc), const), pl.BlockSpec((1, xbc), const),
                  pl.BlockSpec((1, LANES), const), pl.BlockSpec((1, LANES), const),
                  pl.BlockSpec((1, d_mix), const), pl.BlockSpec((1, d_mix), const),
                  pl.BlockSpec((None, SUBLANES, xbc), per_b), pl.BlockSpec((None, hp_rows, SSD_STATE), per_b)],
        out_specs=[pl.BlockSpec((tc, d_mix), lambda b, c: (b * nchunk + c, 0)),
                   pl.BlockSpec((None, hp_rows, SSD_STATE), per_b)],
        out_shape=[jax.ShapeDtypeStruct((nb * nchunk * tc, d_mix), BF16),
                   jax.ShapeDtypeStruct((nb, hp_rows, SSD_STATE), F32)],
        scratch_shapes=[pltpu.VMEM((tc + SUBLANES, xbc), F32), pltpu.VMEM((tc, xbc), F32)],
        compiler_params=_params(("parallel", "arbitrary")),
    )(zx, dtr, p["ssd_conv_w"], p["ssd_conv_b"], p["ssd_dt_bias"], p["ssd_a_log"], p["ssd_d_full"],
      p["ssd_norm"], buf0, h0)


CONV_HALO = 32


def _conv_kernel(u_ref, buf0_ref, w_ref, b_ref, g_ref, beta_ref, o_ref, ext_ref, acc_ref, *, tt, kw, rb, lb):
    i = pl.program_id(1)
    ch = u_ref.shape[1]

    @pl.when(i == 0)
    def _():
        ext_ref[0:CONV_HALO, :] = buf0_ref[...]

    ext_ref[CONV_HALO:CONV_HALO + tt, :] = u_ref[...]
    base = CONV_HALO - (kw - 1)

    def lane_block(cb, carry):
        cols = pl.ds(pl.multiple_of(cb * lb, lb), lb)
        for rblk in range(tt // rb):
            t0 = rblk * rb
            acc = jnp.broadcast_to(b_ref[:, cols], (rb, lb))
            for r in range(SUBLANES):
                taps = [(a, SUBLANES * a + r - base) for a in range((kw + base) // SUBLANES + 1)
                        if 0 <= SUBLANES * a + r - base < kw]
                win = rb if r == 0 else rb + SUBLANES
                z = None
                for a, k in taps:
                    term = w_ref[k:k + 1, cols] * ext_ref[pl.ds(t0 + SUBLANES * a, win), cols]
                    z = term if z is None else z + term
                acc = acc + (z if r == 0 else pltpu.roll(z, win - r, 0)[:rb])
            acc_ref[pl.ds(t0, rb), cols] = acc
        return carry

    lax.fori_loop(0, ch // lb, lane_block, 0)
    ext_ref[0:CONV_HALO, :] = ext_ref[tt:tt + CONV_HALO, :]

    x = acc_ref[...]
    mu = jnp.mean(x, axis=-1, keepdims=True)
    xc = x - mu
    var = jnp.mean(xc * xc, axis=-1, keepdims=True)
    y = xc * lax.rsqrt(var + EPS) * g_ref[...] + beta_ref[...]
    o_ref[...] = (y * _sigmoid(y)).astype(o_ref.dtype)


def _conv(u, p, buf0, *, nb, ntile, tt, row_block0):
    ch = u.shape[1]
    kw = p["conv_w"].shape[0]
    rb = min(tt, 64)
    lb = LANES
    const = lambda b, i: (0, 0)
    kern = functools.partial(_conv_kernel, tt=tt, kw=kw, rb=rb, lb=lb)
    return pl.pallas_call(
        kern,
        grid=(nb, ntile),
        in_specs=[pl.BlockSpec((tt, ch), lambda b, i: (row_block0 + b * ntile + i, 0)),
                  pl.BlockSpec((None, CONV_HALO, ch), lambda b, i: (b, 0, 0)),
                  pl.BlockSpec((kw, ch), const), pl.BlockSpec((1, ch), const),
                  pl.BlockSpec((1, ch), const), pl.BlockSpec((1, ch), const)],
        out_specs=pl.BlockSpec((tt, ch), lambda b, i: (b * ntile + i, 0)),
        out_shape=jax.ShapeDtypeStruct((nb * ntile * tt, ch), BF16),
        scratch_shapes=[pltpu.VMEM((tt + CONV_HALO, ch), F32), pltpu.VMEM((tt, ch), F32)],
        compiler_params=_params(("parallel", "arbitrary")),
    )(u, buf0, p["conv_w"], p["conv_b"], p["conv_ln_g"], p["conv_ln_b"])


def _moba_prompt_kernel(q_ref, k_ref, v_ref, o_ref, kmean_ref, kb_ref, vt_ref, qt_ref, sel_ref, acc_ref,
                        *, nblk, hpg, scale):
    i = pl.program_id(2)
    blk = MOBA_BLOCK
    hd = ATT_HEAD_DIM
    nq = hpg * blk
    nch = nq // LANES

    @pl.when(i == 0)
    def _():
        kmean_ref[...] = jnp.mean(k_ref[...].reshape(nblk, blk, hd), axis=1)
        for jb in range(nblk):
            kb_ref[jb] = k_ref[jb * blk:(jb + 1) * blk, :].astype(BF16)
            vt_ref[jb] = v_ref[jb * blk:(jb + 1) * blk, :].T.astype(BF16)

    qt = jnp.concatenate([q_ref[:, hh * hd:(hh + 1) * hd].T for hh in range(hpg)], axis=1)
    qt_ref[...] = qt.astype(BF16)
    gate = jnp.dot(kmean_ref[...], qt, preferred_element_type=F32, precision=lax.Precision.HIGHEST)
    blk_row = lax.broadcasted_iota(jnp.int32, (nblk, nq), 0)
    gate = jnp.where(blk_row < i, gate, NEG_INF)
    rank = jnp.zeros((nblk, nq), jnp.int32)
    for jp in range(nblk):
        gj = gate[jp:jp + 1, :]
        rank = rank + jnp.where(gj > gate, 1, jnp.where((gj == gate) & (blk_row > jp), 1, 0))
    sel = jnp.where((rank < MOBA_TOPK) & (gate > NEG_INF), 1.0, 0.0)
    for jb in range(nblk):
        sel_ref[jb] = jnp.broadcast_to(sel[jb:jb + 1, :], (SUBLANES, nq))

    key_i = lax.broadcasted_iota(jnp.int32, (blk, LANES), 0)
    lane_i = lax.broadcasted_iota(jnp.int32, (blk, LANES), 1)

    def block_update(j, m, l, diagonal):
        kj = kb_ref[j]
        vtj = vt_ref[j]
        m_out, l_out = [], []
        for c in range(nch):
            cs = slice(c * LANES, (c + 1) * LANES)
            s = _dot(kj, qt_ref[:, cs]) * scale
            if diagonal:
                q_off = (c * LANES) % blk
                s = jnp.where(key_i <= lane_i + q_off, s, NEG_INF)
                m_new = jnp.max(s, axis=0, keepdims=True)
                p = jnp.exp(s - m_new)
                l_out.append(jnp.sum(p, axis=0, keepdims=True))
                acc_ref[:, cs] = _dot(vtj, p.astype(BF16))
            else:
                s = jnp.where(sel_ref[j, 0:1, cs] > 0.0, s, NEG_INF)
                m_old = m[:, cs]
                m_new = jnp.maximum(m_old, jnp.max(s, axis=0, keepdims=True))
                alpha = jnp.exp(m_old - m_new)
                p = jnp.exp(s - m_new)
                l_out.append(alpha * l[:, cs] + jnp.sum(p, axis=0, keepdims=True))
                acc_ref[:, cs] = alpha * acc_ref[:, cs] + _dot(vtj, p.astype(BF16))
            m_out.append(m_new)
        return jnp.concatenate(m_out, axis=1), jnp.concatenate(l_out, axis=1)

    m0, l0 = block_update(i, None, None, True)
    m, l = lax.fori_loop(0, i, lambda j, ml: block_update(j, ml[0], ml[1], False), (m0, l0))
    out_t = acc_ref[...] / l
    for hh in range(hpg):
        o_ref[:, hh * hd:(hh + 1) * hd] = out_t[:, hh * blk:(hh + 1) * blk].T.astype(o_ref.dtype)


def _moba_prompt(qkv, *, nb, t_len, n_heads, kv_heads):
    hd = ATT_HEAD_DIM
    blk = MOBA_BLOCK
    assert t_len % blk == 0
    nblk = t_len // blk
    hpg = n_heads // kv_heads
    kcol0 = n_heads
    vcol0 = n_heads + kv_heads
    kern = functools.partial(_moba_prompt_kernel, nblk=nblk, hpg=hpg, scale=hd ** -0.5)
    return pl.pallas_call(
        kern,
        grid=(nb, kv_heads, nblk),
        in_specs=[pl.BlockSpec((blk, hpg * hd), lambda b, g, i: (b * nblk + i, g)),
                  pl.BlockSpec((t_len, hd), lambda b, g, i: (b, kcol0 + g)),
                  pl.BlockSpec((t_len, hd), lambda b, g, i: (b, vcol0 + g))],
        out_specs=pl.BlockSpec((blk, hpg * hd), lambda b, g, i: (b * nblk + i, g)),
        out_shape=jax.ShapeDtypeStruct((nb * t_len, n_heads * hd), BF16),
        scratch_shapes=[pltpu.VMEM((nblk, hd), F32), pltpu.VMEM((nblk, blk, hd), BF16),
                        pltpu.VMEM((nblk, hd, blk), BF16), pltpu.VMEM((hd, hpg * blk), BF16),
                        pltpu.VMEM((nblk, SUBLANES, hpg * blk), F32), pltpu.VMEM((hd, hpg * blk), F32)],
        compiler_params=_params(("parallel", "parallel", "arbitrary")),
    )(qkv, qkv, qkv)


def _moba_sample_kernel(pt_ref, q_ref, *refs, nblk, bps, ppb, kv_heads, rows_per_group, dt_len, scale):
    del pt_ref
    npg = bps * ppb
    kpages, vpages = refs[:npg], refs[npg:2 * npg]
    knew_ref, vnew_ref, o_ref, gate_ref, m_ref, l_ref, acc_ref = refs[2 * npg:]
    step = pl.program_id(1)
    hd = ATT_HEAD_DIM
    n_rows = q_ref.shape[0]
    rpg = rows_per_group
    n_keys = MOBA_BLOCK * kv_heads
    q = q_ref[...]
    lane = lax.broadcasted_iota(jnp.int32, (n_rows, LANES), 1)

    @pl.when(step == 0)
    def _():
        gate_ref[...] = jnp.full((n_rows, LANES), NEG_INF, F32)
        m_ref[...] = jnp.zeros((n_rows, LANES), F32)
        l_ref[...] = jnp.zeros((n_rows, LANES), F32)

    key_head = lax.broadcasted_iota(jnp.int32, (n_rows, n_keys), 1) % kv_heads
    row_head = lax.broadcasted_iota(jnp.int32, (n_rows, n_keys), 0) // rpg
    own_head = key_head == row_head
    g_lane = lax.broadcasted_iota(jnp.int32, (n_rows, SUBLANES), 1) % kv_heads
    g_rowh = lax.broadcasted_iota(jnp.int32, (n_rows, SUBLANES), 0) // rpg
    q_bf = q.astype(BF16)
    gate_new, m_new, l_new = gate_ref[...], m_ref[...], l_ref[...]
    for i in range(bps):
        j = step * bps + i
        kblk = jnp.concatenate([r[...] for r in kpages[i * ppb:(i + 1) * ppb]], axis=0)
        vblk = jnp.concatenate([r[...] for r in vpages[i * ppb:(i + 1) * ppb]], axis=0)
        s = _dot_nt(q_bf, kblk.astype(BF16)) * scale
        s = jnp.where(own_head, s, NEG_INF)
        m_col = jnp.max(s, axis=-1, keepdims=True)
        p = jnp.exp(s - m_col)
        l_col = jnp.sum(p, axis=-1, keepdims=True)
        acc_ref[j] = _dot(p.astype(BF16), vblk.astype(BF16))
        ksub = jnp.sum(kblk.reshape(n_keys // SUBLANES, SUBLANES, hd), axis=0) * (1.0 / MOBA_BLOCK)
        gall = _dot_nt(q, ksub, precision=lax.Precision.HIGHEST)
        g_col = jnp.sum(jnp.where(g_lane == g_rowh, gall, 0.0), axis=-1, keepdims=True)
        is_j = lane == j
        gate_new = jnp.where(is_j, g_col, gate_new)
        m_new = jnp.where(is_j, m_col, m_new)
        l_new = jnp.where(is_j, l_col, l_new)
    gate_ref[...] = gate_new
    m_ref[...] = m_new
    l_ref[...] = l_new

    def group_partials(g, kg, vg, mask):
        qg = q_ref[g * rpg:(g + 1) * rpg, :]
        sg = jnp.where(mask, _dot_nt(qg.astype(BF16), kg.astype(BF16)) * scale, NEG_INF)
        mg = jnp.max(sg, axis=-1, keepdims=True)
        pg = jnp.exp(sg - mg)
        return mg, jnp.sum(pg, axis=-1, keepdims=True), _dot(pg.astype(BF16), vg.astype(BF16))

    @pl.when(step == nblk // bps - 1)
    def _():
        work = gate_ref[...]
        sel = jnp.zeros((n_rows, LANES), F32)
        for _ in range(min(MOBA_TOPK, nblk)):
            mx = jnp.max(work, axis=-1, keepdims=True)
            idx = jnp.min(jnp.where(work == mx, lane, LANES), axis=-1, keepdims=True)
            pick = lane == idx
            sel = jnp.where(pick & (mx > NEG_INF), 1.0, sel)
            work = jnp.where(pick, NEG_INF, work)
        row_t = lax.broadcasted_iota(jnp.int32, (rpg, knew_ref.shape[0]), 0) % dt_len
        key_i = lax.broadcasted_iota(jnp.int32, (rpg, knew_ref.shape[0]), 1)
        own = [group_partials(g, knew_ref[:, g * hd:(g + 1) * hd], vnew_ref[:, g * hd:(g + 1) * hd],
                              key_i <= row_t) for g in range(kv_heads)]
        m_own = jnp.concatenate([o[0] for o in own], axis=0)
        l_own = jnp.concatenate([o[1] for o in own], axis=0)
        acc_own = jnp.concatenate([o[2] for o in own], axis=0)
        m_all = m_ref[...]
        picked = sel > 0.0
        m_tot = jnp.maximum(jnp.max(jnp.where(picked, m_all, NEG_INF), axis=-1, keepdims=True), m_own)
        w = jnp.where(picked, jnp.exp(m_all - m_tot), 0.0)
        w_own = jnp.exp(m_own - m_tot)
        l_tot = jnp.sum(w * l_ref[...], axis=-1, keepdims=True) + w_own * l_own
        out = w_own * acc_own
        for jj in range(nblk):
            out = out + w[:, jj:jj + 1] * acc_ref[jj]
        o_ref[...] = out / l_tot


def _moba_sample(q_rows, knew, vnew, cache_k, cache_v, page_table, layer, *, kv_heads, dt_len):
    db, n_rows, hd = q_rows.shape
    page_rows = cache_k.shape[2]
    page = page_rows // kv_heads
    width = kv_heads * hd
    ppb = MOBA_BLOCK // page
    n_pages = page_table.shape[1]
    assert n_pages % ppb == 0 and MOBA_BLOCK % page == 0 and SUBLANES % kv_heads == 0
    nblk = n_pages // ppb
    assert nblk <= LANES
    rows_per_group = n_rows // kv_heads

    bps = _pick_tile(nblk, 4, 1)
    pps = bps * ppb

    def page_spec(o):
        return pl.BlockSpec((None, None, page_rows, hd), lambda b, j, pt: (layer, pt[b, pps * j + o], 0, 0))

    page_specs = [page_spec(o) for o in range(pps)]
    per_b = lambda b, j, pt: (b, 0, 0)
    kern = functools.partial(_moba_sample_kernel, nblk=nblk, bps=bps, ppb=ppb, kv_heads=kv_heads,
                             rows_per_group=rows_per_group, dt_len=dt_len, scale=hd ** -0.5)
    grid_spec = pltpu.PrefetchScalarGridSpec(
        num_scalar_prefetch=1,
        grid=(db, nblk // bps),
        in_specs=[pl.BlockSpec((None, n_rows, hd), per_b)]
                 + page_specs + page_specs
                 + [pl.BlockSpec((None, knew.shape[1], width), per_b)] * 2,
        out_specs=pl.BlockSpec((None, n_rows, hd), per_b),
        scratch_shapes=[pltpu.VMEM((n_rows, LANES), F32)] * 3 + [pltpu.VMEM((nblk, n_rows, hd), F32)],
    )
    return pl.pallas_call(
        kern,
        grid_spec=grid_spec,
        out_shape=jax.ShapeDtypeStruct((db, n_rows, hd), F32),
        compiler_params=_params(("parallel", "arbitrary")),
    )(page_table, q_rows, *([cache_k] * len(page_specs)), *([cache_v] * len(page_specs)), knew, vnew)


def _rope_tables(pos):
    half = ATT_HEAD_DIM // 2
    inv_freq = jnp.exp(-math.log(ROPE_THETA) * jnp.arange(half, dtype=F32) * 2.0 / ATT_HEAD_DIM)
    ang = pos[:, None] * inv_freq[None, :]
    cos, sin = jnp.cos(ang), jnp.sin(ang)
    return jnp.concatenate([cos, cos], axis=-1), jnp.concatenate([-sin, sin], axis=-1)


def _prep_weights(d_model, w_in, w_ssd_out, w_conv_out, w_att_out, w_o, w_up, w_down, conv_ch, n_heads, kv_heads):
    d_mix = d_model // 2
    ssd_heads = d_mix // SSD_HEADDIM
    xbc = d_mix + 2 * SSD_GROUPS * SSD_STATE
    o_dt = d_mix + xbc
    o_conv = o_dt + ssd_heads
    qkv_w = (n_heads + 2 * kv_heads) * ATT_HEAD_DIM
    wt = jnp.swapaxes(w_in, 1, 2)
    return {
        "w_zx": _transpose_cast(wt, 0, o_dt, 512),
        "w_dt": _transpose_cast(wt, o_dt, LANES, LANES),
        "w_rest": _transpose_cast(wt, o_conv, w_in.shape[2] - o_conv, 512),
        "o_val": 0, "o_gate": conv_ch, "o_qkv": 2 * conv_ch, "o_g": 2 * conv_ch + qkv_w,
        "w_branch": jnp.stack([w_ssd_out, w_conv_out, w_att_out]).astype(BF16),
        "w_o": w_o, "w_up": w_up, "w_down": w_down,
    }


def _prep_layer(l, d_mix, ssd_conv_w, ssd_conv_b, ssd_dt_bias, ssd_a_log, ssd_d, ssd_norm,
                conv_w, conv_b, conv_ln_g, conv_ln_b, norm_mix, norm_ffn):
    xbc = ssd_conv_w.shape[2]
    conv_ch = conv_w.shape[2]
    pad_row = lambda v: jnp.pad(v, (0, LANES - v.shape[0])).reshape(1, LANES)
    return {
        "d_mix": d_mix,
        "norm_mix": norm_mix[l], "norm_ffn": norm_ffn[l],
        "ssd_conv_w": ssd_conv_w[l], "ssd_conv_b": ssd_conv_b[l].reshape(1, xbc),
        "ssd_dt_bias": pad_row(ssd_dt_bias[l]), "ssd_a_log": pad_row(ssd_a_log[l]),
        "ssd_d_full": jnp.repeat(ssd_d[l], SSD_HEADDIM).reshape(1, d_mix),
        "ssd_norm": ssd_norm[l].reshape(1, d_mix),
        "conv_w": conv_w[l], "conv_b": conv_b[l].reshape(1, conv_ch),
        "conv_ln_g": conv_ln_g[l].reshape(1, conv_ch), "conv_ln_b": conv_ln_b[l].reshape(1, conv_ch),
    }


def _last_rows(a, nb, t_len, n, col0=0):
    return jnp.stack([a[(b + 1) * t_len - n:(b + 1) * t_len, col0:] for b in range(nb)])


def _layer(x, p, w, l, cos_t, sin_t, cache_k, cache_v, page_table, state_ssm_l, state_ssm_conv_l, state_conv_l,
           *, nbp, t_len, dbs, dt_len, n_heads, kv_heads):
    m, d_model = x.shape
    d_mix = p["d_mix"]
    hd = ATT_HEAD_DIM
    bt = nbp * t_len
    xbc_w = p["ssd_conv_w"].shape[1]
    conv_ch = p["conv_w"].shape[1]
    kv_w = kv_heads * hd

    h = _rmsnorm(x, p["norm_mix"], BF16)
    zx = _matmul(h, w["w_zx"], l)
    dtr = _matmul(h, w["w_dt"], l, tn=LANES)
    u = _matmul_glu(h, w["w_rest"], l, w["o_val"], w["o_gate"], conv_ch)
    n_rope_tiles = (n_heads + kv_heads) * hd // 512
    qkv = _matmul(h, w["w_rest"], l, epilogue=functools.partial(_ep_rope, n_rope_tiles=n_rope_tiles),
                  extra=(cos_t, sin_t), extra_specs=(_spec_row_table, _spec_row_table), split_rows=True,
                  col0=w["o_qkv"], n=w["o_g"] - w["o_qkv"])
    gates = _matmul(h, w["w_rest"], l, epilogue=_ep_sigmoid, out_dtype=BF16, split_rows=True, col0=w["o_g"])

    tc = 128
    ncp = t_len // tc
    y_p, ssm_p = _ssd(zx, dtr, p, jnp.zeros((nbp, SUBLANES, xbc_w), F32),
                      jnp.zeros((nbp, d_mix, SSD_STATE), F32),
                      nb=nbp, nchunk=ncp, tc=tc, t_valid=tc, row_block0=0)
    zx_s = zx[bt:].reshape(dbs, dt_len, -1)
    pad_t = ((0, 0), (0, tc - dt_len), (0, 0))
    zx_sp = jnp.pad(zx_s, pad_t).reshape(dbs * tc, -1)
    dtr_sp = jnp.pad(dtr[bt:].reshape(dbs, dt_len, LANES), pad_t).reshape(dbs * tc, LANES)
    kc = state_ssm_conv_l.shape[1]
    buf0_s = jnp.pad(state_ssm_conv_l, ((0, 0), (SUBLANES - kc, 0), (0, 0)))
    y_s, ssm_s = _ssd(zx_sp, dtr_sp, p, buf0_s, state_ssm_l.reshape(dbs, d_mix, SSD_STATE),
                      nb=dbs, nchunk=1, tc=tc, t_valid=dt_len, row_block0=0)
    y_s = y_s.reshape(dbs, tc, d_mix)[:, :dt_len].reshape(dbs * dt_len, d_mix)
    ssm_buf_p = _last_rows(zx, nbp, t_len, kc, d_mix)
    ssm_buf_s = jnp.concatenate([state_ssm_conv_l, zx_s[:, :, d_mix:]], axis=1)[:, -kc:]

    tt = 256
    cw1 = p["conv_w"].shape[0] - 1
    c_p = _conv(u, p, jnp.zeros((nbp, CONV_HALO, conv_ch), F32), nb=nbp, ntile=t_len // tt, tt=tt, row_block0=0)
    buf0_c = jnp.pad(state_conv_l, ((0, 0), (CONV_HALO - cw1, 0), (0, 0)))
    c_s = _conv(u, p, buf0_c, nb=dbs, ntile=1, tt=dt_len, row_block0=bt // dt_len)
    conv_buf_p = _last_rows(u, nbp, t_len, cw1)
    conv_buf_s = jnp.concatenate([state_conv_l, u[bt:].reshape(dbs, dt_len, conv_ch)], axis=1)[:, -cw1:]

    a_p = _moba_prompt(qkv, nb=nbp, t_len=t_len, n_heads=n_heads, kv_heads=kv_heads)
    qkv_s = qkv[bt:]
    q_rows = qkv_s[:, :n_heads * hd].reshape(dbs, dt_len, n_heads, hd).transpose(0, 2, 1, 3)
    q_rows = q_rows.reshape(dbs, n_heads * dt_len, hd)
    k_s = qkv_s[:, n_heads * hd:n_heads * hd + kv_w].reshape(dbs, dt_len, kv_w)
    v_s = qkv_s[:, n_heads * hd + kv_w:].reshape(dbs, dt_len, kv_w)
    pad_k = ((0, 0), (0, LANES - dt_len), (0, 0))
    a_s = _moba_sample(q_rows, jnp.pad(k_s, pad_k), jnp.pad(v_s, pad_k), cache_k, cache_v, page_table, l,
                       kv_heads=kv_heads, dt_len=dt_len)
    a_s = a_s.reshape(dbs, n_heads, dt_len, hd).transpose(0, 2, 1, 3).reshape(dbs * dt_len, n_heads * hd)
    k_p = qkv[:bt, n_heads * hd:n_heads * hd + kv_w].reshape(nbp, t_len, kv_heads, hd)
    v_p = qkv[:bt, n_heads * hd + kv_w:].reshape(nbp, t_len, kv_heads, hd)

    y3 = jnp.stack([jnp.concatenate([y_p, y_s]), jnp.concatenate([c_p, c_s]),
                    jnp.concatenate([a_p, a_s.astype(BF16)])])
    merged = _merge(y3, w["w_branch"], gates, l)
    res_spec = (_spec_out_tile,)
    x = _matmul(merged, w["w_o"], l, epilogue=_ep_residual, extra=(x,), extra_specs=res_spec)
    hf = _rmsnorm(x, p["norm_ffn"], BF16)
    act = _matmul(hf, w["w_up"], l, epilogue=_ep_relu2, out_dtype=BF16)
    x = _matmul(act, w["w_down"], l, epilogue=_ep_residual, extra=(x,), extra_specs=res_spec, tk=4096)
    states = (k_p, v_p, k_s.reshape(dbs, dt_len, kv_heads, hd), v_s.reshape(dbs, dt_len, kv_heads, hd),
              ssm_p.reshape(nbp, -1, SSD_HEADDIM, SSD_STATE), ssm_s.reshape(dbs, -1, SSD_HEADDIM, SSD_STATE),
              ssm_buf_p, ssm_buf_s, conv_buf_p, conv_buf_s)
    return x, states


def kernel(x_prompt, x_sample, cache_k, cache_v, page_table, state_ssm, state_ssm_conv, state_conv, norm_mix, w_in, ssd_conv_w, ssd_conv_b, ssd_dt_bias, ssd_a_log, ssd_d, ssd_norm, w_ssd_out, conv_w, conv_b, conv_ln_g, conv_ln_b, w_conv_out, w_att_out, w_o, norm_ffn, w_up, w_down, norm_final):
    nbp, t_len, d_model = x_prompt.shape
    dbs, dt_len, _ = x_sample.shape
    depth = w_in.shape[0]
    n_phys, page, kv_heads, hd = cache_k.shape[1:]
    assert hd == ATT_HEAD_DIM
    n_heads = w_att_out.shape[1] // hd
    past_len = page_table.shape[1] * page
    assert past_len % MOBA_BLOCK == 0 and dt_len <= SUBLANES
    bt = nbp * t_len

    x = jnp.concatenate([x_prompt.reshape(bt, d_model), x_sample.reshape(dbs * dt_len, d_model)], axis=0)
    pos = jnp.concatenate([jnp.tile(jnp.arange(t_len, dtype=F32), nbp),
                           jnp.tile(past_len + jnp.arange(dt_len, dtype=F32), dbs)])
    cos_t, sin_t = _rope_tables(pos)
    ck = cache_k.reshape(depth, n_phys, page * kv_heads, hd)
    cv = cache_v.reshape(depth, n_phys, page * kv_heads, hd)
    w = _prep_weights(d_model, w_in, w_ssd_out, w_conv_out, w_att_out, w_o, w_up, w_down, conv_w.shape[2],
                      n_heads, kv_heads)

    per_layer = []
    for l in range(depth):
        p = _prep_layer(l, d_model // 2, ssd_conv_w, ssd_conv_b, ssd_dt_bias, ssd_a_log, ssd_d, ssd_norm,
                        conv_w, conv_b, conv_ln_g, conv_ln_b, norm_mix, norm_ffn)
        x, st = _layer(x, p, w, l, cos_t, sin_t, ck, cv, page_table, state_ssm[l], state_ssm_conv[l],
                       state_conv[l], nbp=nbp, t_len=t_len, dbs=dbs, dt_len=dt_len, n_heads=n_heads,
                       kv_heads=kv_heads)
        per_layer.append(st)
    y_prompt = _rmsnorm(x, norm_final, F32, 0, bt).reshape(nbp, t_len, d_model)
    y_sample = _rmsnorm(x, norm_final, F32, bt, dbs * dt_len).reshape(dbs, dt_len, d_model)
    stacked = [jnp.stack([st[i] for st in per_layer]) for i in range(10)]
    return (y_prompt, y_sample, *stacked)
```

```python
import functools
import math

import jax
import jax.numpy as jnp
from jax import lax
from jax.experimental import pallas as pl
from jax.experimental.pallas import tpu as pltpu

F32 = jnp.float32
BF16 = jnp.bfloat16

SSD_HEADDIM = 64
SSD_STATE = 128
SSD_GROUPS = 4
ATT_HEAD_DIM = 128
MOBA_BLOCK = 256
MOBA_TOPK = 3
ROPE_THETA = 10000.0
EPS = 1e-6

LANES = 128
SUBLANES = 8
BF16_ROWS = 16
VMEM_LIMIT = 56 * 1024 * 1024

NEG_INF = float("-inf")


def _pick_tile(n, cap, mult):
    best = None
    for d in range(mult, min(n, cap) + 1, mult):
        if n % d == 0:
            best = d
    return n if best is None else best


def _params(sem):
    return pltpu.CompilerParams(dimension_semantics=sem, vmem_limit_bytes=VMEM_LIMIT)


def _sigmoid(x):
    return jax.nn.sigmoid(x)


def _dot(a, b):
    return jnp.dot(a, b, preferred_element_type=F32)


def _dot_nt(a, b, precision=None):
    return lax.dot_general(a, b, (((1,), (1,)), ((), ())), preferred_element_type=F32,
                           precision=precision)


def _rmsnorm_kernel(x_ref, g_ref, o_ref):
    x = x_ref[...]
    ms = jnp.mean(x * x, axis=-1, keepdims=True)
    o_ref[...] = (x * lax.rsqrt(ms + EPS) * g_ref[...]).astype(o_ref.dtype)


def _rmsnorm(x, g, out_dtype, row0=0, rows=None):
    d = x.shape[1]
    rows = x.shape[0] if rows is None else rows
    tr = _pick_tile(math.gcd(rows, row0) if row0 else rows, 256, 16)
    blk0 = row0 // tr
    return pl.pallas_call(
        _rmsnorm_kernel,
        grid=(rows // tr,),
        in_specs=[pl.BlockSpec((tr, d), lambda i: (blk0 + i, 0)), pl.BlockSpec((1, d), lambda i: (0, 0))],
        out_specs=pl.BlockSpec((tr, d), lambda i: (i, 0)),
        out_shape=jax.ShapeDtypeStruct((rows, d), out_dtype),
        compiler_params=_params(("parallel",)),
    )(x, g.reshape(1, d))


def _ep_store(acc, o_ref):
    o_ref[...] = acc.astype(o_ref.dtype)


def _ep_sigmoid(acc, o_ref):
    o_ref[...] = _sigmoid(acc).astype(o_ref.dtype)


def _ep_relu2(acc, o_ref):
    r = jnp.maximum(acc, 0.0)
    o_ref[...] = (r * r).astype(o_ref.dtype)


def _ep_residual(acc, o_ref, res_ref):
    o_ref[...] = (res_ref[...] + acc).astype(o_ref.dtype)


def _ep_rope(acc, o_ref, cos_ref, sin_ref, *, n_rope_tiles):
    j = pl.program_id(1)
    c = cos_ref[...]
    s = sin_ref[...]
    do_rope = j < n_rope_tiles
    for h in range(acc.shape[1] // ATT_HEAD_DIM):
        a = acc[:, h * ATT_HEAD_DIM:(h + 1) * ATT_HEAD_DIM]
        partner = pltpu.roll(a, ATT_HEAD_DIM // 2, 1)
        o_ref[:, h * ATT_HEAD_DIM:(h + 1) * ATT_HEAD_DIM] = jnp.where(do_rope, a * c + partner * s, a)


def _row_chunks(tm):
    if tm % (2 * BF16_ROWS):
        return [(0, tm)]
    return [(0, tm // 2), (tm // 2, tm // 2)]


CAST_K = 1024


def _x_dot_w(x_ref, rows, w_ref, w_t=False):
    if w_ref.dtype == BF16:
        return _dot(x_ref[rows, :], w_ref[...])
    acc = None
    kdim = w_ref.shape[1] if w_t else w_ref.shape[0]
    for k0 in range(0, kdim, CAST_K):
        if w_t:
            part = _dot_nt(x_ref[rows, k0:k0 + CAST_K], w_ref[:, k0:k0 + CAST_K].astype(BF16))
        else:
            part = _dot(x_ref[rows, k0:k0 + CAST_K], w_ref[k0:k0 + CAST_K, :].astype(BF16))
        acc = part if acc is None else acc + part
    return acc


def _mm_kernel(*refs, nk, n_extra, epilogue, split_rows, w_t=False):
    x_ref, w_ref = refs[0], refs[1]
    extra = refs[2:2 + n_extra]
    o_ref = refs[2 + n_extra]
    tm = x_ref.shape[0]
    if nk == 1:
        for r0, rc in (_row_chunks(tm) if split_rows else [(0, tm)]):
            rows = pl.ds(r0, rc)
            epilogue(_x_dot_w(x_ref, rows, w_ref, w_t), o_ref.at[rows], *[e.at[rows] for e in extra])
        return
    acc_ref = refs[3 + n_extra]
    k = pl.program_id(2)
    all_rows = pl.ds(0, tm)

    @pl.when(k == 0)
    def _():
        acc_ref[...] = _x_dot_w(x_ref, all_rows, w_ref)

    @pl.when(k > 0)
    def _():
        acc_ref[...] += _x_dot_w(x_ref, all_rows, w_ref)

    @pl.when(k == nk - 1)
    def _():
        epilogue(acc_ref[...], o_ref, *extra)


def _matmul(x, w, layer, *, epilogue=_ep_store, out_dtype=F32, tm_cap=1376, tn=512, tk=None,
            extra=(), extra_specs=(), split_rows=False, col0=0, n=None):
    m, kdim = x.shape
    n = w.shape[2] - col0 if n is None else n
    tm = _pick_tile(m, tm_cap, 16)
    tn = min(tn, n)
    tk = kdim if tk is None else tk
    assert n % tn == 0 and kdim % tk == 0 and col0 % tn == 0 and col0 + n <= w.shape[2]
    cb0 = col0 // tn
    nk = kdim // tk
    grid = (m // tm, n // tn, nk)
    kern = functools.partial(_mm_kernel, nk=nk, n_extra=len(extra), epilogue=epilogue, split_rows=split_rows)
    scratch = [] if nk == 1 else [pltpu.VMEM((tm, tn), F32)]
    return pl.pallas_call(
        kern,
        grid=grid,
        in_specs=[pl.BlockSpec((tm, tk), lambda i, j, k: (i, k)),
                  pl.BlockSpec((None, tk, tn), lambda i, j, k: (layer, k, cb0 + j))]
                 + [s(tm, tn) for s in extra_specs],
        out_specs=pl.BlockSpec((tm, tn), lambda i, j, k: (i, j)),
        out_shape=jax.ShapeDtypeStruct((m, n), out_dtype),
        scratch_shapes=scratch,
        compiler_params=_params(("parallel", "parallel", "arbitrary")),
    )(x, w, *extra)


def _matmul_t(x, wt, layer, row0, n, *, epilogue=_ep_store, out_dtype=F32, tm_cap=1376, tn=512,
              extra=(), extra_specs=(), split_rows=False):
    m, kdim = x.shape
    tm = _pick_tile(m, tm_cap, 16)
    tn = min(tn, n)
    assert n % tn == 0 and row0 % SUBLANES == 0 and wt.shape[2] == kdim
    kern = functools.partial(_mm_kernel, nk=1, n_extra=len(extra), epilogue=epilogue, split_rows=split_rows,
                             w_t=True)
    return pl.pallas_call(
        kern,
        grid=(m // tm, n // tn, 1),
        in_specs=[pl.BlockSpec((tm, kdim), lambda i, j, k: (i, 0)),
                  pl.BlockSpec((pl.Squeezed(), pl.Element(tn), pl.Element(kdim)),
                               lambda i, j, k: (layer, pl.multiple_of(row0 + j * tn, SUBLANES), 0))]
                 + [s(tm, tn) for s in extra_specs],
        out_specs=pl.BlockSpec((tm, tn), lambda i, j, k: (i, j)),
        out_shape=jax.ShapeDtypeStruct((m, n), out_dtype),
        compiler_params=_params(("parallel", "parallel", "arbitrary")),
    )(x, wt, *extra)


def _glu_kernel(x_ref, wv_ref, wg_ref, o_ref):
    rows = pl.ds(0, x_ref.shape[0])
    val = _x_dot_w(x_ref, rows, wv_ref, True)
    o_ref[...] = (val * _sigmoid(_x_dot_w(x_ref, rows, wg_ref, True))).astype(o_ref.dtype)


def _matmul_glu_t(x, wt, layer, row_val, row_gate, n, *, tm_cap=1376, tn=256):
    m, kdim = x.shape
    tm = _pick_tile(m, tm_cap, 16)
    assert n % tn == 0 and row_val % SUBLANES == 0 and row_gate % SUBLANES == 0

    def wspec(row0):
        return pl.BlockSpec((pl.Squeezed(), pl.Element(tn), pl.Element(kdim)),
                            lambda i, j: (layer, pl.multiple_of(row0 + j * tn, SUBLANES), 0))

    return pl.pallas_call(
        _glu_kernel,
        grid=(m // tm, n // tn),
        in_specs=[pl.BlockSpec((tm, kdim), lambda i, j: (i, 0)), wspec(row_val), wspec(row_gate)],
        out_specs=pl.BlockSpec((tm, tn), lambda i, j: (i, j)),
        out_shape=jax.ShapeDtypeStruct((m, n), F32),
        compiler_params=_params(("parallel", "parallel")),
    )(x, wt, wt)


def _spec_row_table(tm, tn):
    return pl.BlockSpec((tm, ATT_HEAD_DIM), lambda i, j, k: (i, 0))


def _spec_out_tile(tm, tn):
    return pl.BlockSpec((tm, tn), lambda i, j, k: (i, j))


def _merge_kernel(y_ref, w_ref, g_ref, o_ref, tot_ref):
    b = pl.program_id(2)
    chunks = _row_chunks(y_ref.shape[0])

    def gated(rows):
        return g_ref[rows, :].astype(F32) * _dot(y_ref[rows, :], w_ref[...])

    @pl.when(b == 0)
    def _():
        for r0, rc in chunks:
            rows = pl.ds(r0, rc)
            tot_ref[rows, :] = gated(rows)

    @pl.when(b > 0)
    def _():
        for r0, rc in chunks:
            rows = pl.ds(r0, rc)
            tot = tot_ref[rows, :] + gated(rows)
            tot_ref[rows, :] = tot
            o_ref[rows, :] = tot.astype(o_ref.dtype)


def _merge(y3, w3, g, layer, tm_cap=1376, tn=1024):
    nb, m, kb = y3.shape
    d = w3.shape[3]
    tm = _pick_tile(m, tm_cap, 16)
    ncol = d // tn
    return pl.pallas_call(
        _merge_kernel,
        grid=(m // tm, ncol, nb),
        in_specs=[pl.BlockSpec((None, tm, kb), lambda i, j, b: (b, i, 0)),
                  pl.BlockSpec((None, None, kb, tn), lambda i, j, b: (b, layer, 0, j)),
                  pl.BlockSpec((tm, tn), lambda i, j, b: (i, b * ncol + j))],
        out_specs=pl.BlockSpec((tm, tn), lambda i, j, b: (i, j)),
        out_shape=jax.ShapeDtypeStruct((m, d), BF16),
        scratch_shapes=[pltpu.VMEM((tm, tn), F32)],
        compiler_params=_params(("parallel", "parallel", "arbitrary")),
    )(y3, w3, g)


def _ssd_kernel(zx_ref, dt_ref, cw_ref, cb_ref, dtb_ref, alog_ref, dfull_ref, nw_ref, buf0_ref, h0_ref,
                y_ref, h_ref, ext_ref, act_ref, *, tc, t_valid, d_mix, kw):
    c = pl.program_id(1)
    n_state = SSD_STATE
    pair = 2 * SSD_HEADDIM
    grp_w = d_mix // SSD_GROUPS
    pairs_per_group = grp_w // pair

    @pl.when(c == 0)
    def _():
        h_ref[...] = h0_ref[...]
        ext_ref[0:SUBLANES, :] = buf0_ref[...]

    ext_ref[SUBLANES:SUBLANES + tc, :] = zx_ref[:, d_mix:]
    conv = cb_ref[...] + cw_ref[0:1, :] * ext_ref[SUBLANES - (kw - 1):SUBLANES - (kw - 1) + tc, :]
    for k in range(1, kw):
        off = SUBLANES - (kw - 1) + k
        conv = conv + cw_ref[k:k + 1, :] * ext_ref[off:off + tc, :]
    ext_ref[0:SUBLANES, :] = ext_ref[tc:tc + SUBLANES, :]
    act_ref[...] = conv * _sigmoid(conv)

    dt = jax.nn.softplus(dt_ref[...] + dtb_ref[...])
    row = lax.broadcasted_iota(jnp.int32, (tc, LANES), 0)
    if t_valid < tc:
        dt = jnp.where(row < t_valid, dt, 0.0)
    a = -jnp.exp(alog_ref[...])
    da = dt * a
    r_i = lax.broadcasted_iota(jnp.int32, (tc, tc), 0)
    c_i = lax.broadcasted_iota(jnp.int32, (tc, tc), 1)
    causal = c_i <= r_i
    tri = jnp.where(causal, 1.0, 0.0).astype(BF16)
    da_hi = da.astype(BF16)
    rem = da - da_hi.astype(F32)
    da_mid = rem.astype(BF16)
    da_lo = (rem - da_mid.astype(F32)).astype(BF16)
    acum = _dot(tri, da_hi) + _dot(tri, da_mid) + _dot(tri, da_lo)
    acum_t = acum.T
    dt_t = dt.T
    a_tot = acum[tc - 1:tc, :]
    to_end = jnp.exp(a_tot - acum) * dt
    eacum = jnp.exp(acum)
    cdec_t = jnp.exp(acum_t[:, tc - 1:tc])

    lane = lax.broadcasted_iota(jnp.int32, (tc, pair), 1)
    lo_lane = lane < SSD_HEADDIM
    prow = lax.broadcasted_iota(jnp.int32, (pair, n_state), 0)
    lo_row = prow < SSD_HEADDIM

    for g in range(SSD_GROUPS):
        b_g = act_ref[:, d_mix + g * n_state:d_mix + (g + 1) * n_state]
        c_g = act_ref[:, d_mix + (SSD_GROUPS + g) * n_state:d_mix + (SSD_GROUPS + g + 1) * n_state]
        c_bf = c_g.astype(BF16)
        cbm = _dot_nt(c_bf, b_g.astype(BF16))
        y_parts = []
        for pr in range(pairs_per_group):
            hp = g * pairs_per_group + pr
            xp = act_ref[:, hp * pair:(hp + 1) * pair]
            xp_bf = xp.astype(BF16)
            xp_t = xp.T.astype(BF16)
            hpair = h_ref[hp * pair:(hp + 1) * pair, :]
            ys, ss = [], []
            for h in (2 * hp, 2 * hp + 1):
                seg = acum[:, h:h + 1] - acum_t[h:h + 1, :]
                wm = jnp.where(causal, jnp.exp(seg), 0.0) * cbm * dt_t[h:h + 1, :]
                ys.append(_dot(wm.astype(BF16), xp_bf))
                bw = (b_g * to_end[:, h:h + 1]).astype(BF16)
                ss.append(_dot(xp_t, bw))
            y_pair = jnp.where(lo_lane, ys[0], ys[1])
            e_pair = jnp.where(lo_lane, eacum[:, 2 * hp:2 * hp + 1], eacum[:, 2 * hp + 1:2 * hp + 2])
            y_pair = y_pair + _dot_nt(c_bf, hpair.astype(BF16)) * e_pair
            s_pair = jnp.where(lo_row, ss[0], ss[1])
            cd_pair = jnp.where(lo_row, cdec_t[2 * hp:2 * hp + 1, :], cdec_t[2 * hp + 1:2 * hp + 2, :])
            h_ref[hp * pair:(hp + 1) * pair, :] = hpair * cd_pair + s_pair
            y_parts.append(y_pair)
        cols = slice(g * grp_w, (g + 1) * grp_w)
        yg = jnp.concatenate(y_parts, axis=1) + dfull_ref[:, cols] * act_ref[:, cols]
        z_g = zx_ref[:, cols]
        yg = yg * (z_g * _sigmoid(z_g))
        ms = jnp.mean(yg * yg, axis=-1, keepdims=True)
        y_ref[:, cols] = (yg * lax.rsqrt(ms + EPS) * nw_ref[:, cols]).astype(y_ref.dtype)


def _ssd(zx, dtr, p, buf0, h0, *, nb, nchunk, tc, t_valid, row_block0):
    d_mix = p["d_mix"]
    wz = zx.shape[1]
    xbc = wz - d_mix
    kw = p["ssd_conv_w"].shape[0]
    hp_rows = h0.shape[1]
    blk = lambda b, c: (row_block0 + b * nchunk + c, 0)
    const = lambda b, c: (0, 0)
    per_b = lambda b, c: (b, 0, 0)
    kern = functools.partial(_ssd_kernel, tc=tc, t_valid=t_valid, d_mix=d_mix, kw=kw)
    return pl.pallas_call(
        kern,
        grid=(nb, nchunk),
        in_specs=[pl.BlockSpec((tc, wz), blk), pl.BlockSpec((tc, LANES), blk),
                  pl.BlockSpec((kw, xbc), const), pl.BlockSpec((1, xbc), const),
                  pl.BlockSpec((1, LANES), const), pl.BlockSpec((1, LANES), const),
                  pl.BlockSpec((1, d_mix), const), pl.BlockSpec((1, d_mix), const),
                  pl.BlockSpec((None, SUBLANES, xbc), per_b), pl.BlockSpec((None, hp_rows, SSD_STATE), per_b)],
        out_specs=[pl.BlockSpec((tc, d_mix), lambda b, c: (b * nchunk + c, 0)),
                   pl.BlockSpec((None, hp_rows, SSD_STATE), per_b)],
        out_shape=[jax.ShapeDtypeStruct((nb * nchunk * tc, d_mix), BF16),
                   jax.ShapeDtypeStruct((nb, hp_rows, SSD_STATE), F32)],
        scratch_shapes=[pltpu.VMEM((tc + SUBLANES, xbc), F32), pltpu.VMEM((tc, xbc), F32)],
        compiler_params=_params(("parallel", "arbitrary")),
    )(zx, dtr, p["ssd_conv_w"], p["ssd_conv_b"], p["ssd_dt_bias"], p["ssd_a_log"], p["ssd_d_full"],
      p["ssd_norm"], buf0, h0)


CONV_HALO = 32


def _conv_kernel(u_ref, buf0_ref, w_ref, b_ref, g_ref, beta_ref, o_ref, ext_ref, acc_ref, *, tt, kw, rb, lb):
    i = pl.program_id(1)
    ch = u_ref.shape[1]

    @pl.when(i == 0)
    def _():
        ext_ref[0:CONV_HALO, :] = buf0_ref[...]

    ext_ref[CONV_HALO:CONV_HALO + tt, :] = u_ref[...]
    base = CONV_HALO - (kw - 1)

    def lane_block(cb, carry):
        cols = pl.ds(pl.multiple_of(cb * lb, lb), lb)
        for rblk in range(tt // rb):
            t0 = rblk * rb
            acc = jnp.broadcast_to(b_ref[:, cols], (rb, lb))
            for r in range(SUBLANES):
                taps = [(a, SUBLANES * a + r - base) for a in range((kw + base) // SUBLANES + 1)
                        if 0 <= SUBLANES * a + r - base < kw]
                win = rb if r == 0 else rb + SUBLANES
                z = None
                for a, k in taps:
                    term = w_ref[k:k + 1, cols] * ext_ref[pl.ds(t0 + SUBLANES * a, win), cols]
                    z = term if z is None else z + term
                acc = acc + (z if r == 0 else pltpu.roll(z, win - r, 0)[:rb])
            acc_ref[pl.ds(t0, rb), cols] = acc
        return carry

    lax.fori_loop(0, ch // lb, lane_block, 0)
    ext_ref[0:CONV_HALO, :] = ext_ref[tt:tt + CONV_HALO, :]

    x = acc_ref[...]
    mu = jnp.mean(x, axis=-1, keepdims=True)
    xc = x - mu
    var = jnp.mean(xc * xc, axis=-1, keepdims=True)
    y = xc * lax.rsqrt(var + EPS) * g_ref[...] + beta_ref[...]
    o_ref[...] = (y * _sigmoid(y)).astype(o_ref.dtype)


def _conv(u, p, buf0, *, nb, ntile, tt, row_block0):
    ch = u.shape[1]
    kw = p["conv_w"].shape[0]
    rb = min(tt, 64)
    lb = LANES
    const = lambda b, i: (0, 0)
    kern = functools.partial(_conv_kernel, tt=tt, kw=kw, rb=rb, lb=lb)
    return pl.pallas_call(
        kern,
        grid=(nb, ntile),
        in_specs=[pl.BlockSpec((tt, ch), lambda b, i: (row_block0 + b * ntile + i, 0)),
                  pl.BlockSpec((None, CONV_HALO, ch), lambda b, i: (b, 0, 0)),
                  pl.BlockSpec((kw, ch), const), pl.BlockSpec((1, ch), const),
                  pl.BlockSpec((1, ch), const), pl.BlockSpec((1, ch), const)],
        out_specs=pl.BlockSpec((tt, ch), lambda b, i: (b * ntile + i, 0)),
        out_shape=jax.ShapeDtypeStruct((nb * ntile * tt, ch), BF16),
        scratch_shapes=[pltpu.VMEM((tt + CONV_HALO, ch), F32), pltpu.VMEM((tt, ch), F32)],
        compiler_params=_params(("parallel", "arbitrary")),
    )(u, buf0, p["conv_w"], p["conv_b"], p["conv_ln_g"], p["conv_ln_b"])


def _moba_prompt_kernel(q_ref, k_ref, v_ref, o_ref, kmean_ref, kb_ref, vt_ref, qt_ref, sel_ref, acc_ref,
                        *, nblk, hpg, scale):
    i = pl.program_id(2)
    blk = MOBA_BLOCK
    hd = ATT_HEAD_DIM
    nq = hpg * blk
    nch = nq // LANES

    @pl.when(i == 0)
    def _():
        kmean_ref[...] = jnp.mean(k_ref[...].reshape(nblk, blk, hd), axis=1)
        for jb in range(nblk):
            kb_ref[jb] = k_ref[jb * blk:(jb + 1) * blk, :].astype(BF16)
            vt_ref[jb] = v_ref[jb * blk:(jb + 1) * blk, :].T.astype(BF16)

    qt = jnp.concatenate([q_ref[:, hh * hd:(hh + 1) * hd].T for hh in range(hpg)], axis=1)
    qt_ref[...] = qt.astype(BF16)
    gate = jnp.dot(kmean_ref[...], qt, preferred_element_type=F32, precision=lax.Precision.HIGHEST)
    blk_row = lax.broadcasted_iota(jnp.int32, (nblk, nq), 0)
    gate = jnp.where(blk_row < i, gate, NEG_INF)
    rank = jnp.zeros((nblk, nq), jnp.int32)
    for jp in range(nblk):
        gj = gate[jp:jp + 1, :]
        rank = rank + jnp.where(gj > gate, 1, jnp.where((gj == gate) & (blk_row > jp), 1, 0))
    sel = jnp.where((rank < MOBA_TOPK) & (gate > NEG_INF), 1.0, 0.0)
    for jb in range(nblk):
        sel_ref[jb] = jnp.broadcast_to(sel[jb:jb + 1, :], (SUBLANES, nq))

    key_i = lax.broadcasted_iota(jnp.int32, (blk, LANES), 0)
    lane_i = lax.broadcasted_iota(jnp.int32, (blk, LANES), 1)

    def block_update(j, m, l, diagonal):
        kj = kb_ref[j]
        vtj = vt_ref[j]
        m_out, l_out = [], []
        for c in range(nch):
            cs = slice(c * LANES, (c + 1) * LANES)
            s = _dot(kj, qt_ref[:, cs]) * scale
            if diagonal:
                q_off = (c * LANES) % blk
                s = jnp.where(key_i <= lane_i + q_off, s, NEG_INF)
                m_new = jnp.max(s, axis=0, keepdims=True)
                p = jnp.exp(s - m_new)
                l_out.append(jnp.sum(p, axis=0, keepdims=True))
                acc_ref[:, cs] = _dot(vtj, p.astype(BF16))
            else:
                s = jnp.where(sel_ref[j, 0:1, cs] > 0.0, s, NEG_INF)
                m_old = m[:, cs]
                m_new = jnp.maximum(m_old, jnp.max(s, axis=0, keepdims=True))
                alpha = jnp.exp(m_old - m_new)
                p = jnp.exp(s - m_new)
                l_out.append(alpha * l[:, cs] + jnp.sum(p, axis=0, keepdims=True))
                acc_ref[:, cs] = alpha * acc_ref[:, cs] + _dot(vtj, p.astype(BF16))
            m_out.append(m_new)
        return jnp.concatenate(m_out, axis=1), jnp.concatenate(l_out, axis=1)

    m0, l0 = block_update(i, None, None, True)
    m, l = lax.fori_loop(0, i, lambda j, ml: block_update(j, ml[0], ml[1], False), (m0, l0))
    out_t = acc_ref[...] / l
    for hh in range(hpg):
        o_ref[:, hh * hd:(hh + 1) * hd] = out_t[:, hh * blk:(hh + 1) * blk].T.astype(o_ref.dtype)


def _moba_prompt(qkv, *, nb, t_len, n_heads, kv_heads):
    hd = ATT_HEAD_DIM
    blk = MOBA_BLOCK
    assert t_len % blk == 0
    nblk = t_len // blk
    hpg = n_heads // kv_heads
    kcol0 = n_heads
    vcol0 = n_heads + kv_heads
    kern = functools.partial(_moba_prompt_kernel, nblk=nblk, hpg=hpg, scale=hd ** -0.5)
    return pl.pallas_call(
        kern,
        grid=(nb, kv_heads, nblk),
        in_specs=[pl.BlockSpec((blk, hpg * hd), lambda b, g, i: (b * nblk + i, g)),
                  pl.BlockSpec((t_len, hd), lambda b, g, i: (b, kcol0 + g)),
                  pl.BlockSpec((t_len, hd), lambda b, g, i: (b, vcol0 + g))],
        out_specs=pl.BlockSpec((blk, hpg * hd), lambda b, g, i: (b * nblk + i, g)),
        out_shape=jax.ShapeDtypeStruct((nb * t_len, n_heads * hd), BF16),
        scratch_shapes=[pltpu.VMEM((nblk, hd), F32), pltpu.VMEM((nblk, blk, hd), BF16),
                        pltpu.VMEM((nblk, hd, blk), BF16), pltpu.VMEM((hd, hpg * blk), BF16),
                        pltpu.VMEM((nblk, SUBLANES, hpg * blk), F32), pltpu.VMEM((hd, hpg * blk), F32)],
        compiler_params=_params(("parallel", "parallel", "arbitrary")),
    )(qkv, qkv, qkv)


def _moba_sample_kernel(pt_ref, q_ref, *refs, nblk, bps, ppb, kv_heads, rows_per_group, dt_len, scale):
    del pt_ref
    npg = bps * ppb
    kpages, vpages = refs[:npg], refs[npg:2 * npg]
    knew_ref, vnew_ref, o_ref, gate_ref, m_ref, l_ref, acc_ref = refs[2 * npg:]
    step = pl.program_id(1)
    hd = ATT_HEAD_DIM
    n_rows = q_ref.shape[0]
    rpg = rows_per_group
    n_keys = MOBA_BLOCK * kv_heads
    q = q_ref[...]
    lane = lax.broadcasted_iota(jnp.int32, (n_rows, LANES), 1)

    @pl.when(step == 0)
    def _():
        gate_ref[...] = jnp.full((n_rows, LANES), NEG_INF, F32)
        m_ref[...] = jnp.zeros((n_rows, LANES), F32)
        l_ref[...] = jnp.zeros((n_rows, LANES), F32)

    key_head = lax.broadcasted_iota(jnp.int32, (n_rows, n_keys), 1) % kv_heads
    row_head = lax.broadcasted_iota(jnp.int32, (n_rows, n_keys), 0) // rpg
    own_head = key_head == row_head
    g_lane = lax.broadcasted_iota(jnp.int32, (n_rows, SUBLANES), 1) % kv_heads
    g_rowh = lax.broadcasted_iota(jnp.int32, (n_rows, SUBLANES), 0) // rpg
    q_bf = q.astype(BF16)
    gate_new, m_new, l_new = gate_ref[...], m_ref[...], l_ref[...]
    for i in range(bps):
        j = step * bps + i
        kblk = jnp.concatenate([r[...] for r in kpages[i * ppb:(i + 1) * ppb]], axis=0)
        vblk = jnp.concatenate([r[...] for r in vpages[i * ppb:(i + 1) * ppb]], axis=0)
        s = _dot_nt(q_bf, kblk.astype(BF16)) * scale
        s = jnp.where(own_head, s, NEG_INF)
        m_col = jnp.max(s, axis=-1, keepdims=True)
        p = jnp.exp(s - m_col)
        l_col = jnp.sum(p, axis=-1, keepdims=True)
        acc_ref[j] = _dot(p.astype(BF16), vblk.astype(BF16))
        ksub = jnp.sum(kblk.reshape(n_keys // SUBLANES, SUBLANES, hd), axis=0) * (1.0 / MOBA_BLOCK)
        gall = _dot_nt(q, ksub, precision=lax.Precision.HIGHEST)
        g_col = jnp.sum(jnp.where(g_lane == g_rowh, gall, 0.0), axis=-1, keepdims=True)
        is_j = lane == j
        gate_new = jnp.where(is_j, g_col, gate_new)
        m_new = jnp.where(is_j, m_col, m_new)
        l_new = jnp.where(is_j, l_col, l_new)
    gate_ref[...] = gate_new
    m_ref[...] = m_new
    l_ref[...] = l_new

    def group_partials(g, kg, vg, mask):
        qg = q_ref[g * rpg:(g + 1) * rpg, :]
        sg = jnp.where(mask, _dot_nt(qg.astype(BF16), kg.astype(BF16)) * scale, NEG_INF)
        mg = jnp.max(sg, axis=-1, keepdims=True)
        pg = jnp.exp(sg - mg)
        return mg, jnp.sum(pg, axis=-1, keepdims=True), _dot(pg.astype(BF16), vg.astype(BF16))

    @pl.when(step == nblk // bps - 1)
    def _():
        work = gate_ref[...]
        sel = jnp.zeros((n_rows, LANES), F32)
        for _ in range(min(MOBA_TOPK, nblk)):
            mx = jnp.max(work, axis=-1, keepdims=True)
            idx = jnp.min(jnp.where(work == mx, lane, LANES), axis=-1, keepdims=True)
            pick = lane == idx
            sel = jnp.where(pick & (mx > NEG_INF), 1.0, sel)
            work = jnp.where(pick, NEG_INF, work)
        row_t = lax.broadcasted_iota(jnp.int32, (rpg, knew_ref.shape[0]), 0) % dt_len
        key_i = lax.broadcasted_iota(jnp.int32, (rpg, knew_ref.shape[0]), 1)
        own = [group_partials(g, knew_ref[:, g * hd:(g + 1) * hd], vnew_ref[:, g * hd:(g + 1) * hd],
                              key_i <= row_t) for g in range(kv_heads)]
        m_own = jnp.concatenate([o[0] for o in own], axis=0)
        l_own = jnp.concatenate([o[1] for o in own], axis=0)
        acc_own = jnp.concatenate([o[2] for o in own], axis=0)
        m_all = m_ref[...]
        picked = sel > 0.0
        m_tot = jnp.maximum(jnp.max(jnp.where(picked, m_all, NEG_INF), axis=-1, keepdims=True), m_own)
        w = jnp.where(picked, jnp.exp(m_all - m_tot), 0.0)
        w_own = jnp.exp(m_own - m_tot)
        l_tot = jnp.sum(w * l_ref[...], axis=-1, keepdims=True) + w_own * l_own
        out = w_own * acc_own
        for jj in range(nblk):
            out = out + w[:, jj:jj + 1] * acc_ref[jj]
        o_ref[...] = out / l_tot


def _moba_sample(q_rows, knew, vnew, cache_k, cache_v, page_table, layer, *, kv_heads, dt_len):
    db, n_rows, hd = q_rows.shape
    page_rows = cache_k.shape[2]
    page = page_rows // kv_heads
    width = kv_heads * hd
    ppb = MOBA_BLOCK // page
    n_pages = page_table.shape[1]
    assert n_pages % ppb == 0 and MOBA_BLOCK % page == 0 and SUBLANES % kv_heads == 0
    nblk = n_pages // ppb
    assert nblk <= LANES
    rows_per_group = n_rows // kv_heads

    bps = _pick_tile(nblk, 4, 1)
    pps = bps * ppb

    def page_spec(o):
        return pl.BlockSpec((None, None, page_rows, hd), lambda b, j, pt: (layer, pt[b, pps * j + o], 0, 0))

    page_specs = [page_spec(o) for o in range(pps)]
    per_b = lambda b, j, pt: (b, 0, 0)
    kern = functools.partial(_moba_sample_kernel, nblk=nblk, bps=bps, ppb=ppb, kv_heads=kv_heads,
                             rows_per_group=rows_per_group, dt_len=dt_len, scale=hd ** -0.5)
    grid_spec = pltpu.PrefetchScalarGridSpec(
        num_scalar_prefetch=1,
        grid=(db, nblk // bps),
        in_specs=[pl.BlockSpec((None, n_rows, hd), per_b)]
                 + page_specs + page_specs
                 + [pl.BlockSpec((None, knew.shape[1], width), per_b)] * 2,
        out_specs=pl.BlockSpec((None, n_rows, hd), per_b),
        scratch_shapes=[pltpu.VMEM((n_rows, LANES), F32)] * 3 + [pltpu.VMEM((nblk, n_rows, hd), F32)],
    )
    return pl.pallas_call(
        kern,
        grid_spec=grid_spec,
        out_shape=jax.ShapeDtypeStruct((db, n_rows, hd), F32),
        compiler_params=_params(("parallel", "arbitrary")),
    )(page_table, q_rows, *([cache_k] * len(page_specs)), *([cache_v] * len(page_specs)), knew, vnew)


def _rope_tables(pos):
    half = ATT_HEAD_DIM // 2
    inv_freq = jnp.exp(-math.log(ROPE_THETA) * jnp.arange(half, dtype=F32) * 2.0 / ATT_HEAD_DIM)
    ang = pos[:, None] * inv_freq[None, :]
    cos, sin = jnp.cos(ang), jnp.sin(ang)
    return jnp.concatenate([cos, cos], axis=-1), jnp.concatenate([-sin, sin], axis=-1)


def _prep_weights(d_model, w_in, w_ssd_out, w_conv_out, w_att_out, w_o, w_up, w_down, conv_ch, n_heads, kv_heads):
    d_mix = d_model // 2
    ssd_heads = d_mix // SSD_HEADDIM
    xbc = d_mix + 2 * SSD_GROUPS * SSD_STATE
    o_dt = d_mix + xbc
    o_conv = o_dt + ssd_heads
    qkv_w = (n_heads + 2 * kv_heads) * ATT_HEAD_DIM
    wt = jnp.swapaxes(w_in, 1, 2)
    return {
        "wt": wt, "o_dt": o_dt, "o_val": o_conv, "o_gate": o_conv + conv_ch, "o_qkv": o_conv + 2 * conv_ch,
        "o_g": o_conv + 2 * conv_ch + qkv_w, "n_qkv": qkv_w, "n_g": w_in.shape[2] - (o_conv + 2 * conv_ch + qkv_w),
        "w_branch": jnp.stack([w_ssd_out, w_conv_out, w_att_out]).astype(BF16),
        "w_o": w_o, "w_up": w_up, "w_down": w_down,
    }


def _prep_layer(l, d_mix, ssd_conv_w, ssd_conv_b, ssd_dt_bias, ssd_a_log, ssd_d, ssd_norm,
                conv_w, conv_b, conv_ln_g, conv_ln_b, norm_mix, norm_ffn):
    xbc = ssd_conv_w.shape[2]
    conv_ch = conv_w.shape[2]
    pad_row = lambda v: jnp.pad(v, (0, LANES - v.shape[0])).reshape(1, LANES)
    return {
        "d_mix": d_mix,
        "norm_mix": norm_mix[l], "norm_ffn": norm_ffn[l],
        "ssd_conv_w": ssd_conv_w[l], "ssd_conv_b": ssd_conv_b[l].reshape(1, xbc),
        "ssd_dt_bias": pad_row(ssd_dt_bias[l]), "ssd_a_log": pad_row(ssd_a_log[l]),
        "ssd_d_full": jnp.repeat(ssd_d[l], SSD_HEADDIM).reshape(1, d_mix),
        "ssd_norm": ssd_norm[l].reshape(1, d_mix),
        "conv_w": conv_w[l], "conv_b": conv_b[l].reshape(1, conv_ch),
        "conv_ln_g": conv_ln_g[l].reshape(1, conv_ch), "conv_ln_b": conv_ln_b[l].reshape(1, conv_ch),
    }


def _last_rows(a, nb, t_len, n, col0=0):
    return jnp.stack([a[(b + 1) * t_len - n:(b + 1) * t_len, col0:] for b in range(nb)])


def _layer(x, p, w, l, cos_t, sin_t, cache_k, cache_v, page_table, state_ssm_l, state_ssm_conv_l, state_conv_l,
           *, nbp, t_len, dbs, dt_len, n_heads, kv_heads):
    m, d_model = x.shape
    d_mix = p["d_mix"]
    hd = ATT_HEAD_DIM
    bt = nbp * t_len
    xbc_w = p["ssd_conv_w"].shape[1]
    conv_ch = p["conv_w"].shape[1]
    kv_w = kv_heads * hd

    h = _rmsnorm(x, p["norm_mix"], BF16)
    zx = _matmul_t(h, w["wt"], l, 0, w["o_dt"])
    dtr = _matmul_t(h, w["wt"], l, w["o_dt"], LANES, tn=LANES)
    u = _matmul_glu_t(h, w["wt"], l, w["o_val"], w["o_gate"], conv_ch)
    n_rope_tiles = (n_heads + kv_heads) * hd // 512
    qkv = _matmul_t(h, w["wt"], l, w["o_qkv"], w["n_qkv"],
                    epilogue=functools.partial(_ep_rope, n_rope_tiles=n_rope_tiles),
                    extra=(cos_t, sin_t), extra_specs=(_spec_row_table, _spec_row_table), split_rows=True)
    gates = _matmul_t(h, w["wt"], l, w["o_g"], w["n_g"], epilogue=_ep_sigmoid, out_dtype=BF16, split_rows=True)

    tc = 128
    ncp = t_len // tc
    y_p, ssm_p = _ssd(zx, dtr, p, jnp.zeros((nbp, SUBLANES, xbc_w), F32),
                      jnp.zeros((nbp, d_mix, SSD_STATE), F32),
                      nb=nbp, nchunk=ncp, tc=tc, t_valid=tc, row_block0=0)
    zx_s = zx[bt:].reshape(dbs, dt_len, -1)
    pad_t = ((0, 0), (0, tc - dt_len), (0, 0))
    zx_sp = jnp.pad(zx_s, pad_t).reshape(dbs * tc, -1)
    dtr_sp = jnp.pad(dtr[bt:].reshape(dbs, dt_len, LANES), pad_t).reshape(dbs * tc, LANES)
    kc = state_ssm_conv_l.shape[1]
    buf0_s = jnp.pad(state_ssm_conv_l, ((0, 0), (SUBLANES - kc, 0), (0, 0)))
    y_s, ssm_s = _ssd(zx_sp, dtr_sp, p, buf0_s, state_ssm_l.reshape(dbs, d_mix, SSD_STATE),
                      nb=dbs, nchunk=1, tc=tc, t_valid=dt_len, row_block0=0)
    y_s = y_s.reshape(dbs, tc, d_mix)[:, :dt_len].reshape(dbs * dt_len, d_mix)
    ssm_buf_p = _last_rows(zx, nbp, t_len, kc, d_mix)
    ssm_buf_s = jnp.concatenate([state_ssm_conv_l, zx_s[:, :, d_mix:]], axis=1)[:, -kc:]

    tt = 256
    cw1 = p["conv_w"].shape[0] - 1
    c_p = _conv(u, p, jnp.zeros((nbp, CONV_HALO, conv_ch), F32), nb=nbp, ntile=t_len // tt, tt=tt, row_block0=0)
    buf0_c = jnp.pad(state_conv_l, ((0, 0), (CONV_HALO - cw1, 0), (0, 0)))
    c_s = _conv(u, p, buf0_c, nb=dbs, ntile=1, tt=dt_len, row_block0=bt // dt_len)
    conv_buf_p = _last_rows(u, nbp, t_len, cw1)
    conv_buf_s = jnp.concatenate([state_conv_l, u[bt:].reshape(dbs, dt_len, conv_ch)], axis=1)[:, -cw1:]

    a_p = _moba_prompt(qkv, nb=nbp, t_len=t_len, n_heads=n_heads, kv_heads=kv_heads)
    qkv_s = qkv[bt:]
    q_rows = qkv_s[:, :n_heads * hd].reshape(dbs, dt_len, n_heads, hd).transpose(0, 2, 1, 3)
    q_rows = q_rows.reshape(dbs, n_heads * dt_len, hd)
    k_s = qkv_s[:, n_heads * hd:n_heads * hd + kv_w].reshape(dbs, dt_len, kv_w)
    v_s = qkv_s[:, n_heads * hd + kv_w:].reshape(dbs, dt_len, kv_w)
    pad_k = ((0, 0), (0, LANES - dt_len), (0, 0))
    a_s = _moba_sample(q_rows, jnp.pad(k_s, pad_k), jnp.pad(v_s, pad_k), cache_k, cache_v, page_table, l,
                       kv_heads=kv_heads, dt_len=dt_len)
    a_s = a_s.reshape(dbs, n_heads, dt_len, hd).transpose(0, 2, 1, 3).reshape(dbs * dt_len, n_heads * hd)
    k_p = qkv[:bt, n_heads * hd:n_heads * hd + kv_w].reshape(nbp, t_len, kv_heads, hd)
    v_p = qkv[:bt, n_heads * hd + kv_w:].reshape(nbp, t_len, kv_heads, hd)

    y3 = jnp.stack([jnp.concatenate([y_p, y_s]), jnp.concatenate([c_p, c_s]),
                    jnp.concatenate([a_p, a_s.astype(BF16)])])
    merged = _merge(y3, w["w_branch"], gates, l)
    res_spec = (_spec_out_tile,)
    x = _matmul(merged, w["w_o"], l, epilogue=_ep_residual, extra=(x,), extra_specs=res_spec)
    hf = _rmsnorm(x, p["norm_ffn"], BF16)
    act = _matmul(hf, w["w_up"], l, epilogue=_ep_relu2, out_dtype=BF16)
    x = _matmul(act, w["w_down"], l, epilogue=_ep_residual, extra=(x,), extra_specs=res_spec, tk=4096)
    states = (k_p, v_p, k_s.reshape(dbs, dt_len, kv_heads, hd), v_s.reshape(dbs, dt_len, kv_heads, hd),
              ssm_p.reshape(nbp, -1, SSD_HEADDIM, SSD_STATE), ssm_s.reshape(dbs, -1, SSD_HEADDIM, SSD_STATE),
              ssm_buf_p, ssm_buf_s, conv_buf_p, conv_buf_s)
    return x, states


def kernel(x_prompt, x_sample, cache_k, cache_v, page_table, state_ssm, state_ssm_conv, state_conv, norm_mix, w_in, ssd_conv_w, ssd_conv_b, ssd_dt_bias, ssd_a_log, ssd_d, ssd_norm, w_ssd_out, conv_w, conv_b, conv_ln_g, conv_ln_b, w_conv_out, w_att_out, w_o, norm_ffn, w_up, w_down, norm_final):
    nbp, t_len, d_model = x_prompt.shape
    dbs, dt_len, _ = x_sample.shape
    depth = w_in.shape[0]
    n_phys, page, kv_heads, hd = cache_k.shape[1:]
    assert hd == ATT_HEAD_DIM
    n_heads = w_att_out.shape[1] // hd
    past_len = page_table.shape[1] * page
    assert past_len % MOBA_BLOCK == 0 and dt_len <= SUBLANES
    bt = nbp * t_len

    x = jnp.concatenate([x_prompt.reshape(bt, d_model), x_sample.reshape(dbs * dt_len, d_model)], axis=0)
    pos = jnp.concatenate([jnp.tile(jnp.arange(t_len, dtype=F32), nbp),
                           jnp.tile(past_len + jnp.arange(dt_len, dtype=F32), dbs)])
    cos_t, sin_t = _rope_tables(pos)
    ck = cache_k.reshape(depth, n_phys, page * kv_heads, hd)
    cv = cache_v.reshape(depth, n_phys, page * kv_heads, hd)
    w = _prep_weights(d_model, w_in, w_ssd_out, w_conv_out, w_att_out, w_o, w_up, w_down, conv_w.shape[2],
                      n_heads, kv_heads)

    per_layer = []
    for l in range(depth):
        p = _prep_layer(l, d_model // 2, ssd_conv_w, ssd_conv_b, ssd_dt_bias, ssd_a_log, ssd_d, ssd_norm,
                        conv_w, conv_b, conv_ln_g, conv_ln_b, norm_mix, norm_ffn)
        x, st = _layer(x, p, w, l, cos_t, sin_t, ck, cv, page_table, state_ssm[l], state_ssm_conv[l],
                       state_conv[l], nbp=nbp, t_len=t_len, dbs=dbs, dt_len=dt_len, n_heads=n_heads,
                       kv_heads=kv_heads)
        per_layer.append(st)
    y_prompt = _rmsnorm(x, norm_final, F32, 0, bt).reshape(nbp, t_len, d_model)
    y_sample = _rmsnorm(x, norm_final, F32, bt, dbs * dt_len).reshape(dbs, dt_len, d_model)
    stacked = [jnp.stack([st[i] for st in per_layer]) for i in range(10)]
    return (y_prompt, y_sample, *stacked)
```

```python
import functools
import math

import jax
import jax.numpy as jnp
from jax import lax
from jax.experimental import pallas as pl
from jax.experimental.pallas import tpu as pltpu

F32 = jnp.float32
BF16 = jnp.bfloat16

SSD_HEADDIM = 64
SSD_STATE = 128
SSD_GROUPS = 4
ATT_HEAD_DIM = 128
MOBA_BLOCK = 256
MOBA_TOPK = 3
ROPE_THETA = 10000.0
EPS = 1e-6

LANES = 128
SUBLANES = 8
BF16_ROWS = 16
VMEM_LIMIT = 56 * 1024 * 1024

NEG_INF = float("-inf")


def _pick_tile(n, cap, mult):
    best = None
    for d in range(mult, min(n, cap) + 1, mult):
        if n % d == 0:
            best = d
    return n if best is None else best


def _params(sem):
    return pltpu.CompilerParams(dimension_semantics=sem, vmem_limit_bytes=VMEM_LIMIT)


def _sigmoid(x):
    return jax.nn.sigmoid(x)


def _dot(a, b):
    return jnp.dot(a, b, preferred_element_type=F32)


def _dot_nt(a, b, precision=None):
    return lax.dot_general(a, b, (((1,), (1,)), ((), ())), preferred_element_type=F32,
                           precision=precision)


def _rmsnorm_kernel(x_ref, g_ref, o_ref):
    x = x_ref[...]
    ms = jnp.mean(x * x, axis=-1, keepdims=True)
    o_ref[...] = (x * lax.rsqrt(ms + EPS) * g_ref[...]).astype(o_ref.dtype)


def _rmsnorm(x, g, out_dtype, row0=0, rows=None):
    d = x.shape[1]
    rows = x.shape[0] if rows is None else rows
    tr = _pick_tile(math.gcd(rows, row0) if row0 else rows, 256, 16)
    blk0 = row0 // tr
    return pl.pallas_call(
        _rmsnorm_kernel,
        grid=(rows // tr,),
        in_specs=[pl.BlockSpec((tr, d), lambda i: (blk0 + i, 0)), pl.BlockSpec((1, d), lambda i: (0, 0))],
        out_specs=pl.BlockSpec((tr, d), lambda i: (i, 0)),
        out_shape=jax.ShapeDtypeStruct((rows, d), out_dtype),
        compiler_params=_params(("parallel",)),
    )(x, g.reshape(1, d))


def _ep_store(acc, o_ref):
    o_ref[...] = acc.astype(o_ref.dtype)


def _ep_sigmoid(acc, o_ref):
    o_ref[...] = _sigmoid(acc).astype(o_ref.dtype)


def _ep_relu2(acc, o_ref):
    r = jnp.maximum(acc, 0.0)
    o_ref[...] = (r * r).astype(o_ref.dtype)


def _ep_residual(acc, o_ref, res_ref):
    o_ref[...] = (res_ref[...] + acc).astype(o_ref.dtype)


def _ep_rope(acc, o_ref, cos_ref, sin_ref, *, n_rope_tiles):
    j = pl.program_id(1)
    c = cos_ref[...]
    s = sin_ref[...]
    do_rope = j < n_rope_tiles
    for h in range(acc.shape[1] // ATT_HEAD_DIM):
        a = acc[:, h * ATT_HEAD_DIM:(h + 1) * ATT_HEAD_DIM]
        partner = pltpu.roll(a, ATT_HEAD_DIM // 2, 1)
        o_ref[:, h * ATT_HEAD_DIM:(h + 1) * ATT_HEAD_DIM] = jnp.where(do_rope, a * c + partner * s, a)


def _row_chunks(tm):
    if tm % (2 * BF16_ROWS):
        return [(0, tm)]
    return [(0, tm // 2), (tm // 2, tm // 2)]


CAST_K = 1024


def _x_dot_w(x_ref, rows, w_ref, w_t=False):
    if w_ref.dtype == BF16:
        return _dot(x_ref[rows, :], w_ref[...])
    acc = None
    kdim = w_ref.shape[1] if w_t else w_ref.shape[0]
    for k0 in range(0, kdim, CAST_K):
        if w_t:
            part = _dot_nt(x_ref[rows, k0:k0 + CAST_K], w_ref[:, k0:k0 + CAST_K].astype(BF16))
        else:
            part = _dot(x_ref[rows, k0:k0 + CAST_K], w_ref[k0:k0 + CAST_K, :].astype(BF16))
        acc = part if acc is None else acc + part
    return acc


def _mm_kernel(*refs, nk, n_extra, epilogue, split_rows, w_t=False):
    x_ref, w_ref = refs[0], refs[1]
    extra = refs[2:2 + n_extra]
    o_ref = refs[2 + n_extra]
    tm = x_ref.shape[0]
    if nk == 1:
        for r0, rc in (_row_chunks(tm) if split_rows else [(0, tm)]):
            rows = pl.ds(r0, rc)
            epilogue(_x_dot_w(x_ref, rows, w_ref, w_t), o_ref.at[rows], *[e.at[rows] for e in extra])
        return
    acc_ref = refs[3 + n_extra]
    k = pl.program_id(2)
    all_rows = pl.ds(0, tm)

    @pl.when(k == 0)
    def _():
        acc_ref[...] = _x_dot_w(x_ref, all_rows, w_ref)

    @pl.when(k > 0)
    def _():
        acc_ref[...] += _x_dot_w(x_ref, all_rows, w_ref)

    @pl.when(k == nk - 1)
    def _():
        epilogue(acc_ref[...], o_ref, *extra)


def _matmul(x, w, layer, *, epilogue=_ep_store, out_dtype=F32, tm_cap=1376, tn=512, tk=None,
            extra=(), extra_specs=(), split_rows=False, col0=0, n=None):
    m, kdim = x.shape
    n = w.shape[2] - col0 if n is None else n
    tm = _pick_tile(m, tm_cap, 16)
    tn = min(tn, n)
    tk = kdim if tk is None else tk
    assert n % tn == 0 and kdim % tk == 0 and col0 % tn == 0 and col0 + n <= w.shape[2]
    cb0 = col0 // tn
    nk = kdim // tk
    grid = (m // tm, n // tn, nk)
    kern = functools.partial(_mm_kernel, nk=nk, n_extra=len(extra), epilogue=epilogue, split_rows=split_rows)
    scratch = [] if nk == 1 else [pltpu.VMEM((tm, tn), F32)]
    return pl.pallas_call(
        kern,
        grid=grid,
        in_specs=[pl.BlockSpec((tm, tk), lambda i, j, k: (i, k)),
                  pl.BlockSpec((None, tk, tn), lambda i, j, k: (layer, k, cb0 + j))]
                 + [s(tm, tn) for s in extra_specs],
        out_specs=pl.BlockSpec((tm, tn), lambda i, j, k: (i, j)),
        out_shape=jax.ShapeDtypeStruct((m, n), out_dtype),
        scratch_shapes=scratch,
        compiler_params=_params(("parallel", "parallel", "arbitrary")),
    )(x, w, *extra)


def _matmul_t(x, wt, layer, row0, n, *, epilogue=_ep_store, out_dtype=F32, tm_cap=1376, tn=512,
              extra=(), extra_specs=(), split_rows=False):
    m, kdim = x.shape
    tm = _pick_tile(m, tm_cap, 16)
    tn = min(tn, n)
    assert n % tn == 0 and row0 % SUBLANES == 0 and wt.shape[2] == kdim
    kern = functools.partial(_mm_kernel, nk=1, n_extra=len(extra), epilogue=epilogue, split_rows=split_rows,
                             w_t=True)
    return pl.pallas_call(
        kern,
        grid=(m // tm, n // tn, 1),
        in_specs=[pl.BlockSpec((tm, kdim), lambda i, j, k: (i, 0)),
                  pl.BlockSpec((pl.Squeezed(), pl.Element(tn), pl.Element(kdim)),
                               lambda i, j, k: (layer, pl.multiple_of(row0 + j * tn, SUBLANES), 0))]
                 + [s(tm, tn) for s in extra_specs],
        out_specs=pl.BlockSpec((tm, tn), lambda i, j, k: (i, j)),
        out_shape=jax.ShapeDtypeStruct((m, n), out_dtype),
        compiler_params=_params(("parallel", "parallel", "arbitrary")),
    )(x, wt, *extra)


def _glu_kernel(x_ref, wv_ref, wg_ref, o_ref):
    rows = pl.ds(0, x_ref.shape[0])
    val = _x_dot_w(x_ref, rows, wv_ref, True)
    o_ref[...] = (val * _sigmoid(_x_dot_w(x_ref, rows, wg_ref, True))).astype(o_ref.dtype)


def _matmul_glu_t(x, wt, layer, row_val, row_gate, n, *, tm_cap=1376, tn=256):
    m, kdim = x.shape
    tm = _pick_tile(m, tm_cap, 16)
    assert n % tn == 0 and row_val % SUBLANES == 0 and row_gate % SUBLANES == 0

    def wspec(row0):
        return pl.BlockSpec((pl.Squeezed(), pl.Element(tn), pl.Element(kdim)),
                            lambda i, j: (layer, pl.multiple_of(row0 + j * tn, SUBLANES), 0))

    return pl.pallas_call(
        _glu_kernel,
        grid=(m // tm, n // tn),
        in_specs=[pl.BlockSpec((tm, kdim), lambda i, j: (i, 0)), wspec(row_val), wspec(row_gate)],
        out_specs=pl.BlockSpec((tm, tn), lambda i, j: (i, j)),
        out_shape=jax.ShapeDtypeStruct((m, n), F32),
        compiler_params=_params(("parallel", "parallel")),
    )(x, wt, wt)


def _spec_row_table(tm, tn):
    return pl.BlockSpec((tm, ATT_HEAD_DIM), lambda i, j, k: (i, 0))


def _spec_out_tile(tm, tn):
    return pl.BlockSpec((tm, tn), lambda i, j, k: (i, j))


def _merge_kernel(y_ref, *refs):
    w_refs, (g_ref, o_ref, tot_ref) = refs[:-3], refs[-3:]
    b = pl.program_id(2)
    chunks = _row_chunks(y_ref.shape[0])
    last = len(w_refs) - 1
    for bi, w_ref in enumerate(w_refs):
        @pl.when(b == bi)
        def _(bi=bi, w_ref=w_ref):
            for r0, rc in chunks:
                rows = pl.ds(r0, rc)
                contrib = g_ref[rows, :].astype(F32) * _x_dot_w(y_ref, rows, w_ref)
                if bi == 0:
                    tot_ref[rows, :] = contrib
                    continue
                tot = tot_ref[rows, :] + contrib
                tot_ref[rows, :] = tot
                if bi == last:
                    o_ref[rows, :] = tot.astype(o_ref.dtype)


def _merge(y3, ws, g, layer, tm_cap=1376, tn=512):
    nb, m, kb = y3.shape
    d = ws[0].shape[2]
    tm = _pick_tile(m, tm_cap, 16)
    ncol = d // tn
    wspec = pl.BlockSpec((None, kb, tn), lambda i, j, b: (layer, 0, j))
    return pl.pallas_call(
        _merge_kernel,
        grid=(m // tm, ncol, nb),
        in_specs=[pl.BlockSpec((None, tm, kb), lambda i, j, b: (b, i, 0))] + [wspec] * nb
                 + [pl.BlockSpec((tm, tn), lambda i, j, b: (i, b * ncol + j))],
        out_specs=pl.BlockSpec((tm, tn), lambda i, j, b: (i, j)),
        out_shape=jax.ShapeDtypeStruct((m, d), BF16),
        scratch_shapes=[pltpu.VMEM((tm, tn), F32)],
        compiler_params=_params(("parallel", "parallel", "arbitrary")),
    )(y3, *ws, g)


def _ssd_kernel(zx_ref, dt_ref, cw_ref, cb_ref, dtb_ref, alog_ref, dfull_ref, nw_ref, buf0_ref, h0_ref,
                y_ref, h_ref, ext_ref, act_ref, *, tc, t_valid, d_mix, kw):
    c = pl.program_id(1)
    n_state = SSD_STATE
    pair = 2 * SSD_HEADDIM
    grp_w = d_mix // SSD_GROUPS
    pairs_per_group = grp_w // pair

    @pl.when(c == 0)
    def _():
        h_ref[...] = h0_ref[...]
        ext_ref[0:SUBLANES, :] = buf0_ref[...]

    ext_ref[SUBLANES:SUBLANES + tc, :] = zx_ref[:, d_mix:]
    conv = cb_ref[...] + cw_ref[0:1, :] * ext_ref[SUBLANES - (kw - 1):SUBLANES - (kw - 1) + tc, :]
    for k in range(1, kw):
        off = SUBLANES - (kw - 1) + k
        conv = conv + cw_ref[k:k + 1, :] * ext_ref[off:off + tc, :]
    ext_ref[0:SUBLANES, :] = ext_ref[tc:tc + SUBLANES, :]
    act_ref[...] = conv * _sigmoid(conv)

    dt = jax.nn.softplus(dt_ref[...] + dtb_ref[...])
    row = lax.broadcasted_iota(jnp.int32, (tc, LANES), 0)
    if t_valid < tc:
        dt = jnp.where(row < t_valid, dt, 0.0)
    a = -jnp.exp(alog_ref[...])
    da = dt * a
    r_i = lax.broadcasted_iota(jnp.int32, (tc, tc), 0)
    c_i = lax.broadcasted_iota(jnp.int32, (tc, tc), 1)
    causal = c_i <= r_i
    tri = jnp.where(causal, 1.0, 0.0).astype(BF16)
    da_hi = da.astype(BF16)
    rem = da - da_hi.astype(F32)
    da_mid = rem.astype(BF16)
    da_lo = (rem - da_mid.astype(F32)).astype(BF16)
    acum = _dot(tri, da_hi) + _dot(tri, da_mid) + _dot(tri, da_lo)
    acum_t = acum.T
    dt_t = dt.T
    a_tot = acum[tc - 1:tc, :]
    to_end = jnp.exp(a_tot - acum) * dt
    eacum = jnp.exp(acum)
    cdec_t = jnp.exp(acum_t[:, tc - 1:tc])

    lane = lax.broadcasted_iota(jnp.int32, (tc, pair), 1)
    lo_lane = lane < SSD_HEADDIM
    prow = lax.broadcasted_iota(jnp.int32, (pair, n_state), 0)
    lo_row = prow < SSD_HEADDIM

    for g in range(SSD_GROUPS):
        b_g = act_ref[:, d_mix + g * n_state:d_mix + (g + 1) * n_state]
        c_g = act_ref[:, d_mix + (SSD_GROUPS + g) * n_state:d_mix + (SSD_GROUPS + g + 1) * n_state]
        c_bf = c_g.astype(BF16)
        cbm = _dot_nt(c_bf, b_g.astype(BF16))
        y_parts = []
        for pr in range(pairs_per_group):
            hp = g * pairs_per_group + pr
            xp = act_ref[:, hp * pair:(hp + 1) * pair]
            xp_bf = xp.astype(BF16)
            xp_t = xp.T.astype(BF16)
            hpair = h_ref[hp * pair:(hp + 1) * pair, :]
            ys, ss = [], []
            for h in (2 * hp, 2 * hp + 1):
                seg = acum[:, h:h + 1] - acum_t[h:h + 1, :]
                wm = jnp.where(causal, jnp.exp(seg), 0.0) * cbm * dt_t[h:h + 1, :]
                ys.append(_dot(wm.astype(BF16), xp_bf))
                bw = (b_g * to_end[:, h:h + 1]).astype(BF16)
                ss.append(_dot(xp_t, bw))
            y_pair = jnp.where(lo_lane, ys[0], ys[1])
            e_pair = jnp.where(lo_lane, eacum[:, 2 * hp:2 * hp + 1], eacum[:, 2 * hp + 1:2 * hp + 2])
            y_pair = y_pair + _dot_nt(c_bf, hpair.astype(BF16)) * e_pair
            s_pair = jnp.where(lo_row, ss[0], ss[1])
            cd_pair = jnp.where(lo_row, cdec_t[2 * hp:2 * hp + 1, :], cdec_t[2 * hp + 1:2 * hp + 2, :])
            h_ref[hp * pair:(hp + 1) * pair, :] = hpair * cd_pair + s_pair
            y_parts.append(y_pair)
        cols = slice(g * grp_w, (g + 1) * grp_w)
        yg = jnp.concatenate(y_parts, axis=1) + dfull_ref[:, cols] * act_ref[:, cols]
        z_g = zx_ref[:, cols]
        yg = yg * (z_g * _sigmoid(z_g))
        ms = jnp.mean(yg * yg, axis=-1, keepdims=True)
        y_ref[:, cols] = (yg * lax.rsqrt(ms + EPS) * nw_ref[:, cols]).astype(y_ref.dtype)


def _ssd(zx, dtr, p, buf0, h0, *, nb, nchunk, tc, t_valid, row_block0):
    d_mix = p["d_mix"]
    wz = zx.shape[1]
    xbc = wz - d_mix
    kw = p["ssd_conv_w"].shape[0]
    hp_rows = h0.shape[1]
    blk = lambda b, c: (row_block0 + b * nchunk + c, 0)
    const = lambda b, c: (0, 0)
    per_b = lambda b, c: (b, 0, 0)
    kern = functools.partial(_ssd_kernel, tc=tc, t_valid=t_valid, d_mix=d_mix, kw=kw)
    return pl.pallas_call(
        kern,
        grid=(nb, nchunk),
        in_specs=[pl.BlockSpec((tc, wz), blk), pl.BlockSpec((tc, LANES), blk),
                  pl.BlockSpec((kw, xbc), const), pl.BlockSpec((1, xbc), const),
                  pl.BlockSpec((1, LANES), const), pl.BlockSpec((1, LANES), const),
                  pl.BlockSpec((1, d_mix), const), pl.BlockSpec((1, d_mix), const),
                  pl.BlockSpec((None, SUBLANES, xbc), per_b), pl.BlockSpec((None, hp_rows, SSD_STATE), per_b)],
        out_specs=[pl.BlockSpec((tc, d_mix), lambda b, c: (b * nchunk + c, 0)),
                   pl.BlockSpec((None, hp_rows, SSD_STATE), per_b)],
        out_shape=[jax.ShapeDtypeStruct((nb * nchunk * tc, d_mix), BF16),
                   jax.ShapeDtypeStruct((nb, hp_rows, SSD_STATE), F32)],
        scratch_shapes=[pltpu.VMEM((tc + SUBLANES, xbc), F32), pltpu.VMEM((tc, xbc), F32)],
        compiler_params=_params(("parallel", "arbitrary")),
    )(zx, dtr, p["ssd_conv_w"], p["ssd_conv_b"], p["ssd_dt_bias"], p["ssd_a_log"], p["ssd_d_full"],
      p["ssd_norm"], buf0, h0)


CONV_HALO = 32


def _conv_kernel(u_ref, buf0_ref, w_ref, b_ref, g_ref, beta_ref, o_ref, ext_ref, acc_ref, *, tt, kw, rb, lb):
    i = pl.program_id(1)
    ch = u_ref.shape[1]

    @pl.when(i == 0)
    def _():
        ext_ref[0:CONV_HALO, :] = buf0_ref[...]

    ext_ref[CONV_HALO:CONV_HALO + tt, :] = u_ref[...]
    base = CONV_HALO - (kw - 1)

    def lane_block(cb, carry):
        cols = pl.ds(pl.multiple_of(cb * lb, lb), lb)
        for rblk in range(tt // rb):
            t0 = rblk * rb
            acc = jnp.broadcast_to(b_ref[:, cols], (rb, lb))
            for r in range(SUBLANES):
                taps = [(a, SUBLANES * a + r - base) for a in range((kw + base) // SUBLANES + 1)
                        if 0 <= SUBLANES * a + r - base < kw]
                win = rb if r == 0 else rb + SUBLANES
                z = None
                for a, k in taps:
                    term = w_ref[k:k + 1, cols] * ext_ref[pl.ds(t0 + SUBLANES * a, win), cols]
                    z = term if z is None else z + term
                acc = acc + (z if r == 0 else pltpu.roll(z, win - r, 0)[:rb])
            acc_ref[pl.ds(t0, rb), cols] = acc
        return carry

    lax.fori_loop(0, ch // lb, lane_block, 0)
    ext_ref[0:CONV_HALO, :] = ext_ref[tt:tt + CONV_HALO, :]

    x = acc_ref[...]
    mu = jnp.mean(x, axis=-1, keepdims=True)
    xc = x - mu
    var = jnp.mean(xc * xc, axis=-1, keepdims=True)
    y = xc * lax.rsqrt(var + EPS) * g_ref[...] + beta_ref[...]
    o_ref[...] = (y * _sigmoid(y)).astype(o_ref.dtype)


def _conv(u, p, buf0, *, nb, ntile, tt, row_block0):
    ch = u.shape[1]
    kw = p["conv_w"].shape[0]
    rb = min(tt, 64)
    lb = LANES
    const = lambda b, i: (0, 0)
    kern = functools.partial(_conv_kernel, tt=tt, kw=kw, rb=rb, lb=lb)
    return pl.pallas_call(
        kern,
        grid=(nb, ntile),
        in_specs=[pl.BlockSpec((tt, ch), lambda b, i: (row_block0 + b * ntile + i, 0)),
                  pl.BlockSpec((None, CONV_HALO, ch), lambda b, i: (b, 0, 0)),
                  pl.BlockSpec((kw, ch), const), pl.BlockSpec((1, ch), const),
                  pl.BlockSpec((1, ch), const), pl.BlockSpec((1, ch), const)],
        out_specs=pl.BlockSpec((tt, ch), lambda b, i: (b * ntile + i, 0)),
        out_shape=jax.ShapeDtypeStruct((nb * ntile * tt, ch), BF16),
        scratch_shapes=[pltpu.VMEM((tt + CONV_HALO, ch), F32), pltpu.VMEM((tt, ch), F32)],
        compiler_params=_params(("parallel", "arbitrary")),
    )(u, buf0, p["conv_w"], p["conv_b"], p["conv_ln_g"], p["conv_ln_b"])


def _moba_prompt_kernel(q_ref, k_ref, v_ref, o_ref, kmean_ref, kb_ref, vt_ref, qt_ref, sel_ref, acc_ref,
                        *, nblk, hpg, scale):
    i = pl.program_id(2)
    blk = MOBA_BLOCK
    hd = ATT_HEAD_DIM
    nq = hpg * blk
    nch = nq // LANES

    @pl.when(i == 0)
    def _():
        kmean_ref[...] = jnp.mean(k_ref[...].reshape(nblk, blk, hd), axis=1)
        for jb in range(nblk):
            kb_ref[jb] = k_ref[jb * blk:(jb + 1) * blk, :].astype(BF16)
            vt_ref[jb] = v_ref[jb * blk:(jb + 1) * blk, :].T.astype(BF16)

    qt = jnp.concatenate([q_ref[:, hh * hd:(hh + 1) * hd].T for hh in range(hpg)], axis=1)
    qt_ref[...] = qt.astype(BF16)
    gate = jnp.dot(kmean_ref[...], qt, preferred_element_type=F32, precision=lax.Precision.HIGHEST)
    blk_row = lax.broadcasted_iota(jnp.int32, (nblk, nq), 0)
    gate = jnp.where(blk_row < i, gate, NEG_INF)
    rank = jnp.zeros((nblk, nq), jnp.int32)
    for jp in range(nblk):
        gj = gate[jp:jp + 1, :]
        rank = rank + jnp.where(gj > gate, 1, jnp.where((gj == gate) & (blk_row > jp), 1, 0))
    sel = jnp.where((rank < MOBA_TOPK) & (gate > NEG_INF), 1.0, 0.0)
    for jb in range(nblk):
        sel_ref[jb] = jnp.broadcast_to(sel[jb:jb + 1, :], (SUBLANES, nq))

    key_i = lax.broadcasted_iota(jnp.int32, (blk, LANES), 0)
    lane_i = lax.broadcasted_iota(jnp.int32, (blk, LANES), 1)

    def block_update(j, m, l, diagonal):
        kj = kb_ref[j]
        vtj = vt_ref[j]
        m_out, l_out = [], []
        for c in range(nch):
            cs = slice(c * LANES, (c + 1) * LANES)
            s = _dot(kj, qt_ref[:, cs]) * scale
            if diagonal:
                q_off = (c * LANES) % blk
                s = jnp.where(key_i <= lane_i + q_off, s, NEG_INF)
                m_new = jnp.max(s, axis=0, keepdims=True)
                p = jnp.exp(s - m_new)
                l_out.append(jnp.sum(p, axis=0, keepdims=True))
                acc_ref[:, cs] = _dot(vtj, p.astype(BF16))
            else:
                s = jnp.where(sel_ref[j, 0:1, cs] > 0.0, s, NEG_INF)
                m_old = m[:, cs]
                m_new = jnp.maximum(m_old, jnp.max(s, axis=0, keepdims=True))
                alpha = jnp.exp(m_old - m_new)
                p = jnp.exp(s - m_new)
                l_out.append(alpha * l[:, cs] + jnp.sum(p, axis=0, keepdims=True))
                acc_ref[:, cs] = alpha * acc_ref[:, cs] + _dot(vtj, p.astype(BF16))
            m_out.append(m_new)
        return jnp.concatenate(m_out, axis=1), jnp.concatenate(l_out, axis=1)

    m0, l0 = block_update(i, None, None, True)
    m, l = lax.fori_loop(0, i, lambda j, ml: block_update(j, ml[0], ml[1], False), (m0, l0))
    out_t = acc_ref[...] / l
    for hh in range(hpg):
        o_ref[:, hh * hd:(hh + 1) * hd] = out_t[:, hh * blk:(hh + 1) * blk].T.astype(o_ref.dtype)


def _moba_prompt(qkv, *, nb, t_len, n_heads, kv_heads):
    hd = ATT_HEAD_DIM
    blk = MOBA_BLOCK
    assert t_len % blk == 0
    nblk = t_len // blk
    hpg = n_heads // kv_heads
    kcol0 = n_heads
    vcol0 = n_heads + kv_heads
    kern = functools.partial(_moba_prompt_kernel, nblk=nblk, hpg=hpg, scale=hd ** -0.5)
    return pl.pallas_call(
        kern,
        grid=(nb, kv_heads, nblk),
        in_specs=[pl.BlockSpec((blk, hpg * hd), lambda b, g, i: (b * nblk + i, g)),
                  pl.BlockSpec((t_len, hd), lambda b, g, i: (b, kcol0 + g)),
                  pl.BlockSpec((t_len, hd), lambda b, g, i: (b, vcol0 + g))],
        out_specs=pl.BlockSpec((blk, hpg * hd), lambda b, g, i: (b * nblk + i, g)),
        out_shape=jax.ShapeDtypeStruct((nb * t_len, n_heads * hd), BF16),
        scratch_shapes=[pltpu.VMEM((nblk, hd), F32), pltpu.VMEM((nblk, blk, hd), BF16),
                        pltpu.VMEM((nblk, hd, blk), BF16), pltpu.VMEM((hd, hpg * blk), BF16),
                        pltpu.VMEM((nblk, SUBLANES, hpg * blk), F32), pltpu.VMEM((hd, hpg * blk), F32)],
        compiler_params=_params(("parallel", "parallel", "arbitrary")),
    )(qkv, qkv, qkv)


def _moba_sample_kernel(pt_ref, q_ref, *refs, nblk, bps, ppb, kv_heads, rows_per_group, dt_len, scale):
    del pt_ref
    npg = bps * ppb
    kpages, vpages = refs[:npg], refs[npg:2 * npg]
    knew_ref, vnew_ref, o_ref, gate_ref, m_ref, l_ref, acc_ref = refs[2 * npg:]
    step = pl.program_id(1)
    hd = ATT_HEAD_DIM
    n_rows = q_ref.shape[0]
    rpg = rows_per_group
    n_keys = MOBA_BLOCK * kv_heads
    q = q_ref[...]
    lane = lax.broadcasted_iota(jnp.int32, (n_rows, LANES), 1)

    @pl.when(step == 0)
    def _():
        gate_ref[...] = jnp.full((n_rows, LANES), NEG_INF, F32)
        m_ref[...] = jnp.zeros((n_rows, LANES), F32)
        l_ref[...] = jnp.zeros((n_rows, LANES), F32)

    key_head = lax.broadcasted_iota(jnp.int32, (n_rows, n_keys), 1) % kv_heads
    row_head = lax.broadcasted_iota(jnp.int32, (n_rows, n_keys), 0) // rpg
    own_head = key_head == row_head
    g_lane = lax.broadcasted_iota(jnp.int32, (n_rows, SUBLANES), 1) % kv_heads
    g_rowh = lax.broadcasted_iota(jnp.int32, (n_rows, SUBLANES), 0) // rpg
    q_bf = q.astype(BF16)
    gate_new, m_new, l_new = gate_ref[...], m_ref[...], l_ref[...]
    for i in range(bps):
        j = step * bps + i
        kblk = jnp.concatenate([r[...] for r in kpages[i * ppb:(i + 1) * ppb]], axis=0)
        vblk = jnp.concatenate([r[...] for r in vpages[i * ppb:(i + 1) * ppb]], axis=0)
        s = _dot_nt(q_bf, kblk.astype(BF16)) * scale
        s = jnp.where(own_head, s, NEG_INF)
        m_col = jnp.max(s, axis=-1, keepdims=True)
        p = jnp.exp(s - m_col)
        l_col = jnp.sum(p, axis=-1, keepdims=True)
        acc_ref[j] = _dot(p.astype(BF16), vblk.astype(BF16))
        ksub = jnp.sum(kblk.reshape(n_keys // SUBLANES, SUBLANES, hd), axis=0) * (1.0 / MOBA_BLOCK)
        gall = _dot_nt(q, ksub, precision=lax.Precision.HIGHEST)
        g_col = jnp.sum(jnp.where(g_lane == g_rowh, gall, 0.0), axis=-1, keepdims=True)
        is_j = lane == j
        gate_new = jnp.where(is_j, g_col, gate_new)
        m_new = jnp.where(is_j, m_col, m_new)
        l_new = jnp.where(is_j, l_col, l_new)
    gate_ref[...] = gate_new
    m_ref[...] = m_new
    l_ref[...] = l_new

    def group_partials(g, kg, vg, mask):
        qg = q_ref[g * rpg:(g + 1) * rpg, :]
        sg = jnp.where(mask, _dot_nt(qg.astype(BF16), kg.astype(BF16)) * scale, NEG_INF)
        mg = jnp.max(sg, axis=-1, keepdims=True)
        pg = jnp.exp(sg - mg)
        return mg, jnp.sum(pg, axis=-1, keepdims=True), _dot(pg.astype(BF16), vg.astype(BF16))

    @pl.when(step == nblk // bps - 1)
    def _():
        work = gate_ref[...]
        sel = jnp.zeros((n_rows, LANES), F32)
        for _ in range(min(MOBA_TOPK, nblk)):
            mx = jnp.max(work, axis=-1, keepdims=True)
            idx = jnp.min(jnp.where(work == mx, lane, LANES), axis=-1, keepdims=True)
            pick = lane == idx
            sel = jnp.where(pick & (mx > NEG_INF), 1.0, sel)
            work = jnp.where(pick, NEG_INF, work)
        row_t = lax.broadcasted_iota(jnp.int32, (rpg, knew_ref.shape[0]), 0) % dt_len
        key_i = lax.broadcasted_iota(jnp.int32, (rpg, knew_ref.shape[0]), 1)
        own = [group_partials(g, knew_ref[:, g * hd:(g + 1) * hd], vnew_ref[:, g * hd:(g + 1) * hd],
                              key_i <= row_t) for g in range(kv_heads)]
        m_own = jnp.concatenate([o[0] for o in own], axis=0)
        l_own = jnp.concatenate([o[1] for o in own], axis=0)
        acc_own = jnp.concatenate([o[2] for o in own], axis=0)
        m_all = m_ref[...]
        picked = sel > 0.0
        m_tot = jnp.maximum(jnp.max(jnp.where(picked, m_all, NEG_INF), axis=-1, keepdims=True), m_own)
        w = jnp.where(picked, jnp.exp(m_all - m_tot), 0.0)
        w_own = jnp.exp(m_own - m_tot)
        l_tot = jnp.sum(w * l_ref[...], axis=-1, keepdims=True) + w_own * l_own
        out = w_own * acc_own
        for jj in range(nblk):
            out = out + w[:, jj:jj + 1] * acc_ref[jj]
        o_ref[...] = out / l_tot


def _moba_sample(q_rows, knew, vnew, cache_k, cache_v, page_table, layer, *, kv_heads, dt_len):
    db, n_rows, hd = q_rows.shape
    page_rows = cache_k.shape[2]
    page = page_rows // kv_heads
    width = kv_heads * hd
    ppb = MOBA_BLOCK // page
    n_pages = page_table.shape[1]
    assert n_pages % ppb == 0 and MOBA_BLOCK % page == 0 and SUBLANES % kv_heads == 0
    nblk = n_pages // ppb
    assert nblk <= LANES
    rows_per_group = n_rows // kv_heads

    bps = _pick_tile(nblk, 8, 1)
    pps = bps * ppb

    def page_spec(o):
        return pl.BlockSpec((None, None, page_rows, hd), lambda b, j, pt: (layer, pt[b, pps * j + o], 0, 0))

    page_specs = [page_spec(o) for o in range(pps)]
    per_b = lambda b, j, pt: (b, 0, 0)
    kern = functools.partial(_moba_sample_kernel, nblk=nblk, bps=bps, ppb=ppb, kv_heads=kv_heads,
                             rows_per_group=rows_per_group, dt_len=dt_len, scale=hd ** -0.5)
    grid_spec = pltpu.PrefetchScalarGridSpec(
        num_scalar_prefetch=1,
        grid=(db, nblk // bps),
        in_specs=[pl.BlockSpec((None, n_rows, hd), per_b)]
                 + page_specs + page_specs
                 + [pl.BlockSpec((None, knew.shape[1], width), per_b)] * 2,
        out_specs=pl.BlockSpec((None, n_rows, hd), per_b),
        scratch_shapes=[pltpu.VMEM((n_rows, LANES), F32)] * 3 + [pltpu.VMEM((nblk, n_rows, hd), F32)],
    )
    return pl.pallas_call(
        kern,
        grid_spec=grid_spec,
        out_shape=jax.ShapeDtypeStruct((db, n_rows, hd), F32),
        compiler_params=_params(("parallel", "arbitrary")),
    )(page_table, q_rows, *([cache_k] * len(page_specs)), *([cache_v] * len(page_specs)), knew, vnew)


def _rope_tables(pos):
    half = ATT_HEAD_DIM // 2
    inv_freq = jnp.exp(-math.log(ROPE_THETA) * jnp.arange(half, dtype=F32) * 2.0 / ATT_HEAD_DIM)
    ang = pos[:, None] * inv_freq[None, :]
    cos, sin = jnp.cos(ang), jnp.sin(ang)
    return jnp.concatenate([cos, cos], axis=-1), jnp.concatenate([-sin, sin], axis=-1)


def _prep_weights(d_model, w_in, w_ssd_out, w_conv_out, w_att_out, w_o, w_up, w_down, conv_ch, n_heads, kv_heads):
    d_mix = d_model // 2
    ssd_heads = d_mix // SSD_HEADDIM
    xbc = d_mix + 2 * SSD_GROUPS * SSD_STATE
    o_dt = d_mix + xbc
    o_conv = o_dt + ssd_heads
    qkv_w = (n_heads + 2 * kv_heads) * ATT_HEAD_DIM
    wt = jnp.swapaxes(w_in, 1, 2)
    return {
        "wt": wt, "o_dt": o_dt, "o_val": o_conv, "o_gate": o_conv + conv_ch, "o_qkv": o_conv + 2 * conv_ch,
        "o_g": o_conv + 2 * conv_ch + qkv_w, "n_qkv": qkv_w, "n_g": w_in.shape[2] - (o_conv + 2 * conv_ch + qkv_w),
        "w_branch": (w_ssd_out, w_conv_out, w_att_out),
        "w_o": w_o, "w_up": w_up, "w_down": w_down,
    }


def _prep_layer(l, d_mix, ssd_conv_w, ssd_conv_b, ssd_dt_bias, ssd_a_log, ssd_d, ssd_norm,
                conv_w, conv_b, conv_ln_g, conv_ln_b, norm_mix, norm_ffn):
    xbc = ssd_conv_w.shape[2]
    conv_ch = conv_w.shape[2]
    pad_row = lambda v: jnp.pad(v, (0, LANES - v.shape[0])).reshape(1, LANES)
    return {
        "d_mix": d_mix,
        "norm_mix": norm_mix[l], "norm_ffn": norm_ffn[l],
        "ssd_conv_w": ssd_conv_w[l], "ssd_conv_b": ssd_conv_b[l].reshape(1, xbc),
        "ssd_dt_bias": pad_row(ssd_dt_bias[l]), "ssd_a_log": pad_row(ssd_a_log[l]),
        "ssd_d_full": jnp.repeat(ssd_d[l], SSD_HEADDIM).reshape(1, d_mix),
        "ssd_norm": ssd_norm[l].reshape(1, d_mix),
        "conv_w": conv_w[l], "conv_b": conv_b[l].reshape(1, conv_ch),
        "conv_ln_g": conv_ln_g[l].reshape(1, conv_ch), "conv_ln_b": conv_ln_b[l].reshape(1, conv_ch),
    }


def _last_rows(a, nb, t_len, n, col0=0):
    return jnp.stack([a[(b + 1) * t_len - n:(b + 1) * t_len, col0:] for b in range(nb)])


def _layer(x, p, w, l, cos_t, sin_t, cache_k, cache_v, page_table, state_ssm_l, state_ssm_conv_l, state_conv_l,
           *, nbp, t_len, dbs, dt_len, n_heads, kv_heads):
    m, d_model = x.shape
    d_mix = p["d_mix"]
    hd = ATT_HEAD_DIM
    bt = nbp * t_len
    xbc_w = p["ssd_conv_w"].shape[1]
    conv_ch = p["conv_w"].shape[1]
    kv_w = kv_heads * hd

    h = _rmsnorm(x, p["norm_mix"], BF16)
    zx = _matmul_t(h, w["wt"], l, 0, w["o_dt"])
    dtr = _matmul_t(h, w["wt"], l, w["o_dt"], LANES, tn=LANES)
    u = _matmul_glu_t(h, w["wt"], l, w["o_val"], w["o_gate"], conv_ch)
    n_rope_tiles = (n_heads + kv_heads) * hd // 512
    qkv = _matmul_t(h, w["wt"], l, w["o_qkv"], w["n_qkv"],
                    epilogue=functools.partial(_ep_rope, n_rope_tiles=n_rope_tiles),
                    extra=(cos_t, sin_t), extra_specs=(_spec_row_table, _spec_row_table), split_rows=True)
    gates = _matmul_t(h, w["wt"], l, w["o_g"], w["n_g"], epilogue=_ep_sigmoid, out_dtype=BF16, split_rows=True)

    tc = 128
    ncp = t_len // tc
    y_p, ssm_p = _ssd(zx, dtr, p, jnp.zeros((nbp, SUBLANES, xbc_w), F32),
                      jnp.zeros((nbp, d_mix, SSD_STATE), F32),
                      nb=nbp, nchunk=ncp, tc=tc, t_valid=tc, row_block0=0)
    zx_s = zx[bt:].reshape(dbs, dt_len, -1)
    pad_t = ((0, 0), (0, tc - dt_len), (0, 0))
    zx_sp = jnp.pad(zx_s, pad_t).reshape(dbs * tc, -1)
    dtr_sp = jnp.pad(dtr[bt:].reshape(dbs, dt_len, LANES), pad_t).reshape(dbs * tc, LANES)
    kc = state_ssm_conv_l.shape[1]
    buf0_s = jnp.pad(state_ssm_conv_l, ((0, 0), (SUBLANES - kc, 0), (0, 0)))
    y_s, ssm_s = _ssd(zx_sp, dtr_sp, p, buf0_s, state_ssm_l.reshape(dbs, d_mix, SSD_STATE),
                      nb=dbs, nchunk=1, tc=tc, t_valid=dt_len, row_block0=0)
    y_s = y_s.reshape(dbs, tc, d_mix)[:, :dt_len].reshape(dbs * dt_len, d_mix)
    ssm_buf_p = _last_rows(zx, nbp, t_len, kc, d_mix)
    ssm_buf_s = jnp.concatenate([state_ssm_conv_l, zx_s[:, :, d_mix:]], axis=1)[:, -kc:]

    tt = 256
    cw1 = p["conv_w"].shape[0] - 1
    c_p = _conv(u, p, jnp.zeros((nbp, CONV_HALO, conv_ch), F32), nb=nbp, ntile=t_len // tt, tt=tt, row_block0=0)
    buf0_c = jnp.pad(state_conv_l, ((0, 0), (CONV_HALO - cw1, 0), (0, 0)))
    c_s = _conv(u, p, buf0_c, nb=dbs, ntile=1, tt=dt_len, row_block0=bt // dt_len)
    conv_buf_p = _last_rows(u, nbp, t_len, cw1)
    conv_buf_s = jnp.concatenate([state_conv_l, u[bt:].reshape(dbs, dt_len, conv_ch)], axis=1)[:, -cw1:]

    a_p = _moba_prompt(qkv, nb=nbp, t_len=t_len, n_heads=n_heads, kv_heads=kv_heads)
    qkv_s = qkv[bt:]
    q_rows = qkv_s[:, :n_heads * hd].reshape(dbs, dt_len, n_heads, hd).transpose(0, 2, 1, 3)
    q_rows = q_rows.reshape(dbs, n_heads * dt_len, hd)
    k_s = qkv_s[:, n_heads * hd:n_heads * hd + kv_w].reshape(dbs, dt_len, kv_w)
    v_s = qkv_s[:, n_heads * hd + kv_w:].reshape(dbs, dt_len, kv_w)
    pad_k = ((0, 0), (0, LANES - dt_len), (0, 0))
    a_s = _moba_sample(q_rows, jnp.pad(k_s, pad_k), jnp.pad(v_s, pad_k), cache_k, cache_v, page_table, l,
                       kv_heads=kv_heads, dt_len=dt_len)
    a_s = a_s.reshape(dbs, n_heads, dt_len, hd).transpose(0, 2, 1, 3).reshape(dbs * dt_len, n_heads * hd)
    k_p = qkv[:bt, n_heads * hd:n_heads * hd + kv_w].reshape(nbp, t_len, kv_heads, hd)
    v_p = qkv[:bt, n_heads * hd + kv_w:].reshape(nbp, t_len, kv_heads, hd)

    y3 = jnp.stack([jnp.concatenate([y_p, y_s]), jnp.concatenate([c_p, c_s]),
                    jnp.concatenate([a_p, a_s.astype(BF16)])])
    merged = _merge(y3, w["w_branch"], gates, l)
    res_spec = (_spec_out_tile,)
    x = _matmul(merged, w["w_o"], l, epilogue=_ep_residual, extra=(x,), extra_specs=res_spec)
    hf = _rmsnorm(x, p["norm_ffn"], BF16)
    act = _matmul(hf, w["w_up"], l, epilogue=_ep_relu2, out_dtype=BF16)
    x = _matmul(act, w["w_down"], l, epilogue=_ep_residual, extra=(x,), extra_specs=res_spec, tk=4096)
    states = (k_p, v_p, k_s.reshape(dbs, dt_len, kv_heads, hd), v_s.reshape(dbs, dt_len, kv_heads, hd),
              ssm_p.reshape(nbp, -1, SSD_HEADDIM, SSD_STATE), ssm_s.reshape(dbs, -1, SSD_HEADDIM, SSD_STATE),
              ssm_buf_p, ssm_buf_s, conv_buf_p, conv_buf_s)
    return x, states


def kernel(x_prompt, x_sample, cache_k, cache_v, page_table, state_ssm, state_ssm_conv, state_conv, norm_mix, w_in, ssd_conv_w, ssd_conv_b, ssd_dt_bias, ssd_a_log, ssd_d, ssd_norm, w_ssd_out, conv_w, conv_b, conv_ln_g, conv_ln_b, w_conv_out, w_att_out, w_o, norm_ffn, w_up, w_down, norm_final):
    nbp, t_len, d_model = x_prompt.shape
    dbs, dt_len, _ = x_sample.shape
    depth = w_in.shape[0]
    n_phys, page, kv_heads, hd = cache_k.shape[1:]
    assert hd == ATT_HEAD_DIM
    n_heads = w_att_out.shape[1] // hd
    past_len = page_table.shape[1] * page
    assert past_len % MOBA_BLOCK == 0 and dt_len <= SUBLANES
    bt = nbp * t_len

    x = jnp.concatenate([x_prompt.reshape(bt, d_model), x_sample.reshape(dbs * dt_len, d_model)], axis=0)
    pos = jnp.concatenate([jnp.tile(jnp.arange(t_len, dtype=F32), nbp),
                           jnp.tile(past_len + jnp.arange(dt_len, dtype=F32), dbs)])
    cos_t, sin_t = _rope_tables(pos)
    ck = cache_k.reshape(depth, n_phys, page * kv_heads, hd)
    cv = cache_v.reshape(depth, n_phys, page * kv_heads, hd)
    w = _prep_weights(d_model, w_in, w_ssd_out, w_conv_out, w_att_out, w_o, w_up, w_down, conv_w.shape[2],
                      n_heads, kv_heads)

    per_layer = []
    for l in range(depth):
        p = _prep_layer(l, d_model // 2, ssd_conv_w, ssd_conv_b, ssd_dt_bias, ssd_a_log, ssd_d, ssd_norm,
                        conv_w, conv_b, conv_ln_g, conv_ln_b, norm_mix, norm_ffn)
        x, st = _layer(x, p, w, l, cos_t, sin_t, ck, cv, page_table, state_ssm[l], state_ssm_conv[l],
                       state_conv[l], nbp=nbp, t_len=t_len, dbs=dbs, dt_len=dt_len, n_heads=n_heads,
                       kv_heads=kv_heads)
        per_layer.append(st)
    y_prompt = _rmsnorm(x, norm_final, F32, 0, bt).reshape(nbp, t_len, d_model)
    y_sample = _rmsnorm(x, norm_final, F32, bt, dbs * dt_len).reshape(dbs, dt_len, d_model)
    stacked = [jnp.stack([st[i] for st in per_layer]) for i in range(10)]
    return (y_prompt, y_sample, *stacked)
```

```python
import functools
import math

import jax
import jax.numpy as jnp
from jax import lax
from jax.experimental import pallas as pl
from jax.experimental.pallas import tpu as pltpu

F32 = jnp.float32
BF16 = jnp.bfloat16

SSD_HEADDIM = 64
SSD_STATE = 128
SSD_GROUPS = 4
ATT_HEAD_DIM = 128
MOBA_BLOCK = 256
MOBA_TOPK = 3
ROPE_THETA = 10000.0
EPS = 1e-6

LANES = 128
SUBLANES = 8
BF16_ROWS = 16
VMEM_LIMIT = 56 * 1024 * 1024

NEG_INF = float("-inf")


def _pick_tile(n, cap, mult):
    best = None
    for d in range(mult, min(n, cap) + 1, mult):
        if n % d == 0:
            best = d
    return n if best is None else best


def _params(sem):
    return pltpu.CompilerParams(dimension_semantics=sem, vmem_limit_bytes=VMEM_LIMIT)


def _sigmoid(x):
    return jax.nn.sigmoid(x)


def _dot(a, b):
    return jnp.dot(a, b, preferred_element_type=F32)


def _dot_nt(a, b, precision=None):
    return lax.dot_general(a, b, (((1,), (1,)), ((), ())), preferred_element_type=F32,
                           precision=precision)


def _rmsnorm_kernel(x_ref, g_ref, o_ref):
    x = x_ref[...]
    ms = jnp.mean(x * x, axis=-1, keepdims=True)
    o_ref[...] = (x * lax.rsqrt(ms + EPS) * g_ref[...]).astype(o_ref.dtype)


def _rmsnorm(x, g, out_dtype, row0=0, rows=None):
    d = x.shape[1]
    rows = x.shape[0] if rows is None else rows
    tr = _pick_tile(math.gcd(rows, row0) if row0 else rows, 256, 16)
    blk0 = row0 // tr
    return pl.pallas_call(
        _rmsnorm_kernel,
        grid=(rows // tr,),
        in_specs=[pl.BlockSpec((tr, d), lambda i: (blk0 + i, 0)), pl.BlockSpec((1, d), lambda i: (0, 0))],
        out_specs=pl.BlockSpec((tr, d), lambda i: (i, 0)),
        out_shape=jax.ShapeDtypeStruct((rows, d), out_dtype),
        compiler_params=_params(("parallel",)),
    )(x, g.reshape(1, d))


def _ep_store(acc, o_ref):
    o_ref[...] = acc.astype(o_ref.dtype)


def _ep_sigmoid(acc, o_ref):
    o_ref[...] = _sigmoid(acc).astype(o_ref.dtype)


def _ep_relu2(acc, o_ref):
    r = jnp.maximum(acc, 0.0)
    o_ref[...] = (r * r).astype(o_ref.dtype)


def _ep_residual(acc, o_ref, res_ref):
    o_ref[...] = (res_ref[...] + acc).astype(o_ref.dtype)


def _ep_rope(acc, o_ref, cos_ref, sin_ref, *, n_rope_tiles):
    j = pl.program_id(1)
    c = cos_ref[...]
    s = sin_ref[...]
    do_rope = j < n_rope_tiles
    for h in range(acc.shape[1] // ATT_HEAD_DIM):
        a = acc[:, h * ATT_HEAD_DIM:(h + 1) * ATT_HEAD_DIM]
        partner = pltpu.roll(a, ATT_HEAD_DIM // 2, 1)
        o_ref[:, h * ATT_HEAD_DIM:(h + 1) * ATT_HEAD_DIM] = jnp.where(do_rope, a * c + partner * s, a)


def _row_chunks(tm):
    if tm % (2 * BF16_ROWS):
        return [(0, tm)]
    return [(0, tm // 2), (tm // 2, tm // 2)]


CAST_K = 1024


def _x_dot_w(x_ref, rows, w_ref, w_t=False):
    if w_ref.dtype == BF16:
        return _dot(x_ref[rows, :], w_ref[...])
    acc = None
    kdim = w_ref.shape[1] if w_t else w_ref.shape[0]
    for k0 in range(0, kdim, CAST_K):
        if w_t:
            part = _dot_nt(x_ref[rows, k0:k0 + CAST_K], w_ref[:, k0:k0 + CAST_K].astype(BF16))
        else:
            part = _dot(x_ref[rows, k0:k0 + CAST_K], w_ref[k0:k0 + CAST_K, :].astype(BF16))
        acc = part if acc is None else acc + part
    return acc


def _mm_kernel(*refs, nk, n_extra, epilogue, split_rows, w_t=False):
    x_ref, w_ref = refs[0], refs[1]
    extra = refs[2:2 + n_extra]
    o_ref = refs[2 + n_extra]
    tm = x_ref.shape[0]
    if nk == 1:
        for r0, rc in (_row_chunks(tm) if split_rows else [(0, tm)]):
            rows = pl.ds(r0, rc)
            epilogue(_x_dot_w(x_ref, rows, w_ref, w_t), o_ref.at[rows], *[e.at[rows] for e in extra])
        return
    acc_ref = refs[3 + n_extra]
    k = pl.program_id(2)
    all_rows = pl.ds(0, tm)

    @pl.when(k == 0)
    def _():
        acc_ref[...] = _x_dot_w(x_ref, all_rows, w_ref)

    @pl.when(k > 0)
    def _():
        acc_ref[...] += _x_dot_w(x_ref, all_rows, w_ref)

    @pl.when(k == nk - 1)
    def _():
        epilogue(acc_ref[...], o_ref, *extra)


def _matmul(x, w, layer, *, epilogue=_ep_store, out_dtype=F32, tm_cap=1376, tn=512, tk=None,
            extra=(), extra_specs=(), split_rows=False, col0=0, n=None):
    m, kdim = x.shape
    n = w.shape[2] - col0 if n is None else n
    tm = _pick_tile(m, tm_cap, 16)
    tn = min(tn, n)
    tk = kdim if tk is None else tk
    assert n % tn == 0 and kdim % tk == 0 and col0 % tn == 0 and col0 + n <= w.shape[2]
    cb0 = col0 // tn
    nk = kdim // tk
    grid = (m // tm, n // tn, nk)
    kern = functools.partial(_mm_kernel, nk=nk, n_extra=len(extra), epilogue=epilogue, split_rows=split_rows)
    scratch = [] if nk == 1 else [pltpu.VMEM((tm, tn), F32)]
    return pl.pallas_call(
        kern,
        grid=grid,
        in_specs=[pl.BlockSpec((tm, tk), lambda i, j, k: (i, k)),
                  pl.BlockSpec((None, tk, tn), lambda i, j, k: (layer, k, cb0 + j))]
                 + [s(tm, tn) for s in extra_specs],
        out_specs=pl.BlockSpec((tm, tn), lambda i, j, k: (i, j)),
        out_shape=jax.ShapeDtypeStruct((m, n), out_dtype),
        scratch_shapes=scratch,
        compiler_params=_params(("parallel", "parallel", "arbitrary")),
    )(x, w, *extra)


def _matmul_t(x, wt, layer, row0, n, *, epilogue=_ep_store, out_dtype=F32, tm_cap=1376, tn=512,
              extra=(), extra_specs=(), split_rows=False):
    m, kdim = x.shape
    tm = _pick_tile(m, tm_cap, 16)
    tn = min(tn, n)
    assert n % tn == 0 and row0 % SUBLANES == 0 and wt.shape[2] == kdim
    kern = functools.partial(_mm_kernel, nk=1, n_extra=len(extra), epilogue=epilogue, split_rows=split_rows,
                             w_t=True)
    return pl.pallas_call(
        kern,
        grid=(m // tm, n // tn, 1),
        in_specs=[pl.BlockSpec((tm, kdim), lambda i, j, k: (i, 0)),
                  pl.BlockSpec((pl.Squeezed(), pl.Element(tn), pl.Element(kdim)),
                               lambda i, j, k: (layer, pl.multiple_of(row0 + j * tn, SUBLANES), 0))]
                 + [s(tm, tn) for s in extra_specs],
        out_specs=pl.BlockSpec((tm, tn), lambda i, j, k: (i, j)),
        out_shape=jax.ShapeDtypeStruct((m, n), out_dtype),
        compiler_params=_params(("parallel", "parallel", "arbitrary")),
    )(x, wt, *extra)


def _glu_kernel(x_ref, wv_ref, wg_ref, o_ref):
    rows = pl.ds(0, x_ref.shape[0])
    val = _x_dot_w(x_ref, rows, wv_ref, True)
    o_ref[...] = (val * _sigmoid(_x_dot_w(x_ref, rows, wg_ref, True))).astype(o_ref.dtype)


def _matmul_glu_t(x, wt, layer, row_val, row_gate, n, *, tm_cap=1376, tn=256):
    m, kdim = x.shape
    tm = _pick_tile(m, tm_cap, 16)
    assert n % tn == 0 and row_val % SUBLANES == 0 and row_gate % SUBLANES == 0

    def wspec(row0):
        return pl.BlockSpec((pl.Squeezed(), pl.Element(tn), pl.Element(kdim)),
                            lambda i, j: (layer, pl.multiple_of(row0 + j * tn, SUBLANES), 0))

    return pl.pallas_call(
        _glu_kernel,
        grid=(m // tm, n // tn),
        in_specs=[pl.BlockSpec((tm, kdim), lambda i, j: (i, 0)), wspec(row_val), wspec(row_gate)],
        out_specs=pl.BlockSpec((tm, tn), lambda i, j: (i, j)),
        out_shape=jax.ShapeDtypeStruct((m, n), F32),
        compiler_params=_params(("parallel", "parallel")),
    )(x, wt, wt)


def _spec_row_table(tm, tn):
    return pl.BlockSpec((tm, ATT_HEAD_DIM), lambda i, j, k: (i, 0))


def _spec_out_tile(tm, tn):
    return pl.BlockSpec((tm, tn), lambda i, j, k: (i, j))


def _merge_kernel(y_ref, w_ref, g_ref, o_ref, tot_ref):
    b = pl.program_id(2)
    chunks = _row_chunks(y_ref.shape[0])

    def gated(rows):
        return g_ref[rows, :].astype(F32) * _dot(y_ref[rows, :], w_ref[...])

    @pl.when(b == 0)
    def _():
        for r0, rc in chunks:
            rows = pl.ds(r0, rc)
            tot_ref[rows, :] = gated(rows)

    @pl.when(b > 0)
    def _():
        for r0, rc in chunks:
            rows = pl.ds(r0, rc)
            tot = tot_ref[rows, :] + gated(rows)
            tot_ref[rows, :] = tot
            o_ref[rows, :] = tot.astype(o_ref.dtype)


def _merge(y3, w3, g, layer, tm_cap=1376, tn=1024):
    nb, m, kb = y3.shape
    d = w3.shape[3]
    tm = _pick_tile(m, tm_cap, 16)
    ncol = d // tn
    return pl.pallas_call(
        _merge_kernel,
        grid=(m // tm, ncol, nb),
        in_specs=[pl.BlockSpec((None, tm, kb), lambda i, j, b: (b, i, 0)),
                  pl.BlockSpec((None, None, kb, tn), lambda i, j, b: (b, layer, 0, j)),
                  pl.BlockSpec((tm, tn), lambda i, j, b: (i, b * ncol + j))],
        out_specs=pl.BlockSpec((tm, tn), lambda i, j, b: (i, j)),
        out_shape=jax.ShapeDtypeStruct((m, d), BF16),
        scratch_shapes=[pltpu.VMEM((tm, tn), F32)],
        compiler_params=_params(("parallel", "parallel", "arbitrary")),
    )(y3, w3, g)


def _ssd_kernel(zx_ref, dt_ref, cw_ref, cb_ref, dtb_ref, alog_ref, dfull_ref, nw_ref, buf0_ref, h0_ref,
                y_ref, h_ref, ext_ref, act_ref, *, tc, t_valid, d_mix, kw):
    c = pl.program_id(1)
    n_state = SSD_STATE
    pair = 2 * SSD_HEADDIM
    grp_w = d_mix // SSD_GROUPS
    pairs_per_group = grp_w // pair

    @pl.when(c == 0)
    def _():
        h_ref[...] = h0_ref[...]
        ext_ref[0:SUBLANES, :] = buf0_ref[...]

    ext_ref[SUBLANES:SUBLANES + tc, :] = zx_ref[:, d_mix:]
    conv = cb_ref[...] + cw_ref[0:1, :] * ext_ref[SUBLANES - (kw - 1):SUBLANES - (kw - 1) + tc, :]
    for k in range(1, kw):
        off = SUBLANES - (kw - 1) + k
        conv = conv + cw_ref[k:k + 1, :] * ext_ref[off:off + tc, :]
    ext_ref[0:SUBLANES, :] = ext_ref[tc:tc + SUBLANES, :]
    act_ref[...] = conv * _sigmoid(conv)

    dt = jax.nn.softplus(dt_ref[...] + dtb_ref[...])
    row = lax.broadcasted_iota(jnp.int32, (tc, LANES), 0)
    if t_valid < tc:
        dt = jnp.where(row < t_valid, dt, 0.0)
    a = -jnp.exp(alog_ref[...])
    da = dt * a
    r_i = lax.broadcasted_iota(jnp.int32, (tc, tc), 0)
    c_i = lax.broadcasted_iota(jnp.int32, (tc, tc), 1)
    causal = c_i <= r_i
    tri = jnp.where(causal, 1.0, 0.0).astype(BF16)
    da_hi = da.astype(BF16)
    rem = da - da_hi.astype(F32)
    da_mid = rem.astype(BF16)
    da_lo = (rem - da_mid.astype(F32)).astype(BF16)
    acum = _dot(tri, da_hi) + _dot(tri, da_mid) + _dot(tri, da_lo)
    acum_t = acum.T
    dt_t = dt.T
    a_tot = acum[tc - 1:tc, :]
    to_end = jnp.exp(a_tot - acum) * dt
    eacum = jnp.exp(acum)
    cdec_t = jnp.exp(acum_t[:, tc - 1:tc])

    lane = lax.broadcasted_iota(jnp.int32, (tc, pair), 1)
    lo_lane = lane < SSD_HEADDIM
    prow = lax.broadcasted_iota(jnp.int32, (pair, n_state), 0)
    lo_row = prow < SSD_HEADDIM

    for g in range(SSD_GROUPS):
        b_g = act_ref[:, d_mix + g * n_state:d_mix + (g + 1) * n_state]
        c_g = act_ref[:, d_mix + (SSD_GROUPS + g) * n_state:d_mix + (SSD_GROUPS + g + 1) * n_state]
        c_bf = c_g.astype(BF16)
        cbm = _dot_nt(c_bf, b_g.astype(BF16))
        y_parts = []
        for pr in range(pairs_per_group):
            hp = g * pairs_per_group + pr
            xp = act_ref[:, hp * pair:(hp + 1) * pair]
            xp_bf = xp.astype(BF16)
            xp_t = xp.T.astype(BF16)
            hpair = h_ref[hp * pair:(hp + 1) * pair, :]
            ys, ss = [], []
            for h in (2 * hp, 2 * hp + 1):
                seg = acum[:, h:h + 1] - acum_t[h:h + 1, :]
                wm = jnp.where(causal, jnp.exp(seg), 0.0) * cbm * dt_t[h:h + 1, :]
                ys.append(_dot(wm.astype(BF16), xp_bf))
                bw = (b_g * to_end[:, h:h + 1]).astype(BF16)
                ss.append(_dot(xp_t, bw))
            y_pair = jnp.where(lo_lane, ys[0], ys[1])
            e_pair = jnp.where(lo_lane, eacum[:, 2 * hp:2 * hp + 1], eacum[:, 2 * hp + 1:2 * hp + 2])
            y_pair = y_pair + _dot_nt(c_bf, hpair.astype(BF16)) * e_pair
            s_pair = jnp.where(lo_row, ss[0], ss[1])
            cd_pair = jnp.where(lo_row, cdec_t[2 * hp:2 * hp + 1, :], cdec_t[2 * hp + 1:2 * hp + 2, :])
            h_ref[hp * pair:(hp + 1) * pair, :] = hpair * cd_pair + s_pair
            y_parts.append(y_pair)
        cols = slice(g * grp_w, (g + 1) * grp_w)
        yg = jnp.concatenate(y_parts, axis=1) + dfull_ref[:, cols] * act_ref[:, cols]
        z_g = zx_ref[:, cols]
        yg = yg * (z_g * _sigmoid(z_g))
        ms = jnp.mean(yg * yg, axis=-1, keepdims=True)
        y_ref[:, cols] = (yg * lax.rsqrt(ms + EPS) * nw_ref[:, cols]).astype(y_ref.dtype)


def _ssd(zx, dtr, p, buf0, h0, *, nb, nchunk, tc, t_valid, row_block0):
    d_mix = p["d_mix"]
    wz = zx.shape[1]
    xbc = wz - d_mix
    kw = p["ssd_conv_w"].shape[0]
    hp_rows = h0.shape[1]
    blk = lambda b, c: (row_block0 + b * nchunk + c, 0)
    const = lambda b, c: (0, 0)
    per_b = lambda b, c: (b, 0, 0)
    kern = functools.partial(_ssd_kernel, tc=tc, t_valid=t_valid, d_mix=d_mix, kw=kw)
    return pl.pallas_call(
        kern,
        grid=(nb, nchunk),
        in_specs=[pl.BlockSpec((tc, wz), blk), pl.BlockSpec((tc, LANES), blk),
                  pl.BlockSpec((kw, xbc), const), pl.BlockSpec((1, xbc), const),
                  pl.BlockSpec((1, LANES), const), pl.BlockSpec((1, LANES), const),
                  pl.BlockSpec((1, d_mix), const), pl.BlockSpec((1, d_mix), const),
                  pl.BlockSpec((None, SUBLANES, xbc), per_b), pl.BlockSpec((None, hp_rows, SSD_STATE), per_b)],
        out_specs=[pl.BlockSpec((tc, d_mix), lambda b, c: (b * nchunk + c, 0)),
                   pl.BlockSpec((None, hp_rows, SSD_STATE), per_b)],
        out_shape=[jax.ShapeDtypeStruct((nb * nchunk * tc, d_mix), BF16),
                   jax.ShapeDtypeStruct((nb, hp_rows, SSD_STATE), F32)],
        scratch_shapes=[pltpu.VMEM((tc + SUBLANES, xbc), F32), pltpu.VMEM((tc, xbc), F32)],
        compiler_params=_params(("parallel", "arbitrary")),
    )(zx, dtr, p["ssd_conv_w"], p["ssd_conv_b"], p["ssd_dt_bias"], p["ssd_a_log"], p["ssd_d_full"],
      p["ssd_norm"], buf0, h0)


CONV_HALO = 32


def _conv_kernel(u_ref, buf0_ref, w_ref, b_ref, g_ref, beta_ref, o_ref, ext_ref, acc_ref, *, tt, kw, rb, lb):
    i = pl.program_id(1)
    ch = u_ref.shape[1]

    @pl.when(i == 0)
    def _():
        ext_ref[0:CONV_HALO, :] = buf0_ref[...]

    ext_ref[CONV_HALO:CONV_HALO + tt, :] = u_ref[...]
    base = CONV_HALO - (kw - 1)

    def lane_block(cb, carry):
        cols = pl.ds(pl.multiple_of(cb * lb, lb), lb)
        for rblk in range(tt // rb):
            t0 = rblk * rb
            acc = jnp.broadcast_to(b_ref[:, cols], (rb, lb))
            for r in range(SUBLANES):
                taps = [(a, SUBLANES * a + r - base) for a in range((kw + base) // SUBLANES + 1)
                        if 0 <= SUBLANES * a + r - base < kw]
                win = rb if r == 0 else rb + SUBLANES
                z = None
                for a, k in taps:
                    term = w_ref[k:k + 1, cols] * ext_ref[pl.ds(t0 + SUBLANES * a, win), cols]
                    z = term if z is None else z + term
                acc = acc + (z if r == 0 else pltpu.roll(z, win - r, 0)[:rb])
            acc_ref[pl.ds(t0, rb), cols] = acc
        return carry

    lax.fori_loop(0, ch // lb, lane_block, 0)
    ext_ref[0:CONV_HALO, :] = ext_ref[tt:tt + CONV_HALO, :]

    x = acc_ref[...]
    mu = jnp.mean(x, axis=-1, keepdims=True)
    xc = x - mu
    var = jnp.mean(xc * xc, axis=-1, keepdims=True)
    y = xc * lax.rsqrt(var + EPS) * g_ref[...] + beta_ref[...]
    o_ref[...] = (y * _sigmoid(y)).astype(o_ref.dtype)


def _conv(u, p, buf0, *, nb, ntile, tt, row_block0):
    ch = u.shape[1]
    kw = p["conv_w"].shape[0]
    rb = min(tt, 64)
    lb = LANES
    const = lambda b, i: (0, 0)
    kern = functools.partial(_conv_kernel, tt=tt, kw=kw, rb=rb, lb=lb)
    return pl.pallas_call(
        kern,
        grid=(nb, ntile),
        in_specs=[pl.BlockSpec((tt, ch), lambda b, i: (row_block0 + b * ntile + i, 0)),
                  pl.BlockSpec((None, CONV_HALO, ch), lambda b, i: (b, 0, 0)),
                  pl.BlockSpec((kw, ch), const), pl.BlockSpec((1, ch), const),
                  pl.BlockSpec((1, ch), const), pl.BlockSpec((1, ch), const)],
        out_specs=pl.BlockSpec((tt, ch), lambda b, i: (b * ntile + i, 0)),
        out_shape=jax.ShapeDtypeStruct((nb * ntile * tt, ch), BF16),
        scratch_shapes=[pltpu.VMEM((tt + CONV_HALO, ch), F32), pltpu.VMEM((tt, ch), F32)],
        compiler_params=_params(("parallel", "arbitrary")),
    )(u, buf0, p["conv_w"], p["conv_b"], p["conv_ln_g"], p["conv_ln_b"])


def _moba_prompt_kernel(q_ref, k_ref, v_ref, o_ref, kmean_ref, kb_ref, vt_ref, qt_ref, sel_ref, acc_ref,
                        *, nblk, hpg, scale):
    i = pl.program_id(2)
    blk = MOBA_BLOCK
    hd = ATT_HEAD_DIM
    nq = hpg * blk
    nch = nq // LANES

    @pl.when(i == 0)
    def _():
        kmean_ref[...] = jnp.mean(k_ref[...].reshape(nblk, blk, hd), axis=1)
        for jb in range(nblk):
            kb_ref[jb] = k_ref[jb * blk:(jb + 1) * blk, :].astype(BF16)
            vt_ref[jb] = v_ref[jb * blk:(jb + 1) * blk, :].T.astype(BF16)

    qt = jnp.concatenate([q_ref[:, hh * hd:(hh + 1) * hd].T for hh in range(hpg)], axis=1)
    qt_ref[...] = qt.astype(BF16)
    gate = jnp.dot(kmean_ref[...], qt, preferred_element_type=F32, precision=lax.Precision.HIGHEST)
    blk_row = lax.broadcasted_iota(jnp.int32, (nblk, nq), 0)
    gate = jnp.where(blk_row < i, gate, NEG_INF)
    rank = jnp.zeros((nblk, nq), jnp.int32)
    for jp in range(nblk):
        gj = gate[jp:jp + 1, :]
        rank = rank + jnp.where(gj > gate, 1, jnp.where((gj == gate) & (blk_row > jp), 1, 0))
    sel = jnp.where((rank < MOBA_TOPK) & (gate > NEG_INF), 1.0, 0.0)
    for jb in range(nblk):
        sel_ref[jb] = jnp.broadcast_to(sel[jb:jb + 1, :], (SUBLANES, nq))

    key_i = lax.broadcasted_iota(jnp.int32, (blk, LANES), 0)
    lane_i = lax.broadcasted_iota(jnp.int32, (blk, LANES), 1)

    def block_update(j, m, l, diagonal):
        kj = kb_ref[j]
        vtj = vt_ref[j]
        m_out, l_out = [], []
        for c in range(nch):
            cs = slice(c * LANES, (c + 1) * LANES)
            if diagonal:
                q_off = (c * LANES) % blk
                nk = q_off + LANES
                s = _dot(kb_ref[j, 0:nk, :], qt_ref[:, cs]) * scale
                k_idx = lax.broadcasted_iota(jnp.int32, (nk, LANES), 0)
                q_idx = lax.broadcasted_iota(jnp.int32, (nk, LANES), 1) + q_off
                s = jnp.where(k_idx <= q_idx, s, NEG_INF)
                m_new = jnp.max(s, axis=0, keepdims=True)
                p = jnp.exp(s - m_new)
                l_out.append(jnp.sum(p, axis=0, keepdims=True))
                acc_ref[:, cs] = _dot(vt_ref[j, :, 0:nk], p.astype(BF16))
            else:
                s = _dot(kj, qt_ref[:, cs]) * scale
                s = jnp.where(sel_ref[j, 0:1, cs] > 0.0, s, NEG_INF)
                m_old = m[:, cs]
                m_new = jnp.maximum(m_old, jnp.max(s, axis=0, keepdims=True))
                alpha = jnp.exp(m_old - m_new)
                p = jnp.exp(s - m_new)
                l_out.append(alpha * l[:, cs] + jnp.sum(p, axis=0, keepdims=True))
                acc_ref[:, cs] = alpha * acc_ref[:, cs] + _dot(vtj, p.astype(BF16))
            m_out.append(m_new)
        return jnp.concatenate(m_out, axis=1), jnp.concatenate(l_out, axis=1)

    m0, l0 = block_update(i, None, None, True)
    m, l = lax.fori_loop(0, i, lambda j, ml: block_update(j, ml[0], ml[1], False), (m0, l0))
    out_t = acc_ref[...] / l
    for hh in range(hpg):
        o_ref[:, hh * hd:(hh + 1) * hd] = out_t[:, hh * blk:(hh + 1) * blk].T.astype(o_ref.dtype)


def _moba_prompt(qkv, *, nb, t_len, n_heads, kv_heads):
    hd = ATT_HEAD_DIM
    blk = MOBA_BLOCK
    assert t_len % blk == 0
    nblk = t_len // blk
    hpg = n_heads // kv_heads
    kcol0 = n_heads
    vcol0 = n_heads + kv_heads
    kern = functools.partial(_moba_prompt_kernel, nblk=nblk, hpg=hpg, scale=hd ** -0.5)
    return pl.pallas_call(
        kern,
        grid=(nb, kv_heads, nblk),
        in_specs=[pl.BlockSpec((blk, hpg * hd), lambda b, g, i: (b * nblk + i, g)),
                  pl.BlockSpec((t_len, hd), lambda b, g, i: (b, kcol0 + g)),
                  pl.BlockSpec((t_len, hd), lambda b, g, i: (b, vcol0 + g))],
        out_specs=pl.BlockSpec((blk, hpg * hd), lambda b, g, i: (b * nblk + i, g)),
        out_shape=jax.ShapeDtypeStruct((nb * t_len, n_heads * hd), BF16),
        scratch_shapes=[pltpu.VMEM((nblk, hd), F32), pltpu.VMEM((nblk, blk, hd), BF16),
                        pltpu.VMEM((nblk, hd, blk), BF16), pltpu.VMEM((hd, hpg * blk), BF16),
                        pltpu.VMEM((nblk, SUBLANES, hpg * blk), F32), pltpu.VMEM((hd, hpg * blk), F32)],
        compiler_params=_params(("parallel", "parallel", "arbitrary")),
    )(qkv, qkv, qkv)


def _moba_sample_kernel(pt_ref, q_ref, *refs, nblk, bps, ppb, kv_heads, rows_per_group, dt_len, scale):
    del pt_ref
    npg = bps * ppb
    kpages, vpages = refs[:npg], refs[npg:2 * npg]
    knew_ref, vnew_ref, o_ref, gate_ref, m_ref, l_ref, acc_ref = refs[2 * npg:]
    step = pl.program_id(1)
    hd = ATT_HEAD_DIM
    n_rows = q_ref.shape[0]
    rpg = rows_per_group
    n_keys = MOBA_BLOCK * kv_heads
    q = q_ref[...]
    lane = lax.broadcasted_iota(jnp.int32, (n_rows, LANES), 1)

    @pl.when(step == 0)
    def _():
        gate_ref[...] = jnp.full((n_rows, LANES), NEG_INF, F32)
        m_ref[...] = jnp.zeros((n_rows, LANES), F32)
        l_ref[...] = jnp.zeros((n_rows, LANES), F32)

    key_head = lax.broadcasted_iota(jnp.int32, (n_rows, n_keys), 1) % kv_heads
    row_head = lax.broadcasted_iota(jnp.int32, (n_rows, n_keys), 0) // rpg
    own_head = key_head == row_head
    g_lane = lax.broadcasted_iota(jnp.int32, (n_rows, SUBLANES), 1) % kv_heads
    g_rowh = lax.broadcasted_iota(jnp.int32, (n_rows, SUBLANES), 0) // rpg
    q_bf = q.astype(BF16)
    gate_new, m_new, l_new = gate_ref[...], m_ref[...], l_ref[...]
    for i in range(bps):
        j = step * bps + i
        kblk = jnp.concatenate([r[...] for r in kpages[i * ppb:(i + 1) * ppb]], axis=0)
        vblk = jnp.concatenate([r[...] for r in vpages[i * ppb:(i + 1) * ppb]], axis=0)
        s = _dot_nt(q_bf, kblk.astype(BF16)) * scale
        s = jnp.where(own_head, s, NEG_INF)
        m_col = jnp.max(s, axis=-1, keepdims=True)
        p = jnp.exp(s - m_col)
        l_col = jnp.sum(p, axis=-1, keepdims=True)
        acc_ref[j] = _dot(p.astype(BF16), vblk.astype(BF16))
        ksub = jnp.sum(kblk.reshape(n_keys // SUBLANES, SUBLANES, hd), axis=0) * (1.0 / MOBA_BLOCK)
        gall = _dot_nt(q, ksub, precision=lax.Precision.HIGHEST)
        g_col = jnp.sum(jnp.where(g_lane == g_rowh, gall, 0.0), axis=-1, keepdims=True)
        is_j = lane == j
        gate_new = jnp.where(is_j, g_col, gate_new)
        m_new = jnp.where(is_j, m_col, m_new)
        l_new = jnp.where(is_j, l_col, l_new)
    gate_ref[...] = gate_new
    m_ref[...] = m_new
    l_ref[...] = l_new

    def group_partials(g, kg, vg, mask):
        qg = q_ref[g * rpg:(g + 1) * rpg, :]
        sg = jnp.where(mask, _dot_nt(qg.astype(BF16), kg.astype(BF16)) * scale, NEG_INF)
        mg = jnp.max(sg, axis=-1, keepdims=True)
        pg = jnp.exp(sg - mg)
        return mg, jnp.sum(pg, axis=-1, keepdims=True), _dot(pg.astype(BF16), vg.astype(BF16))

    @pl.when(step == nblk // bps - 1)
    def _():
        work = gate_ref[...]
        sel = jnp.zeros((n_rows, LANES), F32)
        for _ in range(min(MOBA_TOPK, nblk)):
            mx = jnp.max(work, axis=-1, keepdims=True)
            idx = jnp.min(jnp.where(work == mx, lane, LANES), axis=-1, keepdims=True)
            pick = lane == idx
            sel = jnp.where(pick & (mx > NEG_INF), 1.0, sel)
            work = jnp.where(pick, NEG_INF, work)
        row_t = lax.broadcasted_iota(jnp.int32, (rpg, knew_ref.shape[0]), 0) % dt_len
        key_i = lax.broadcasted_iota(jnp.int32, (rpg, knew_ref.shape[0]), 1)
        own = [group_partials(g, knew_ref[:, g * hd:(g + 1) * hd], vnew_ref[:, g * hd:(g + 1) * hd],
                              key_i <= row_t) for g in range(kv_heads)]
        m_own = jnp.concatenate([o[0] for o in own], axis=0)
        l_own = jnp.concatenate([o[1] for o in own], axis=0)
        acc_own = jnp.concatenate([o[2] for o in own], axis=0)
        m_all = m_ref[...]
        picked = sel > 0.0
        m_tot = jnp.maximum(jnp.max(jnp.where(picked, m_all, NEG_INF), axis=-1, keepdims=True), m_own)
        w = jnp.where(picked, jnp.exp(m_all - m_tot), 0.0)
        w_own = jnp.exp(m_own - m_tot)
        l_tot = jnp.sum(w * l_ref[...], axis=-1, keepdims=True) + w_own * l_own
        out = w_own * acc_own
        for jj in range(nblk):
            out = out + w[:, jj:jj + 1] * acc_ref[jj]
        o_ref[...] = out / l_tot


def _moba_sample(q_rows, knew, vnew, cache_k, cache_v, page_table, layer, *, kv_heads, dt_len):
    db, n_rows, hd = q_rows.shape
    page_rows = cache_k.shape[2]
    page = page_rows // kv_heads
    width = kv_heads * hd
    ppb = MOBA_BLOCK // page
    n_pages = page_table.shape[1]
    assert n_pages % ppb == 0 and MOBA_BLOCK % page == 0 and SUBLANES % kv_heads == 0
    nblk = n_pages // ppb
    assert nblk <= LANES
    rows_per_group = n_rows // kv_heads

    bps = _pick_tile(nblk, 8, 1)
    pps = bps * ppb

    def page_spec(o):
        return pl.BlockSpec((None, None, page_rows, hd), lambda b, j, pt: (layer, pt[b, pps * j + o], 0, 0))

    page_specs = [page_spec(o) for o in range(pps)]
    per_b = lambda b, j, pt: (b, 0, 0)
    kern = functools.partial(_moba_sample_kernel, nblk=nblk, bps=bps, ppb=ppb, kv_heads=kv_heads,
                             rows_per_group=rows_per_group, dt_len=dt_len, scale=hd ** -0.5)
    grid_spec = pltpu.PrefetchScalarGridSpec(
        num_scalar_prefetch=1,
        grid=(db, nblk // bps),
        in_specs=[pl.BlockSpec((None, n_rows, hd), per_b)]
                 + page_specs + page_specs
                 + [pl.BlockSpec((None, knew.shape[1], width), per_b)] * 2,
        out_specs=pl.BlockSpec((None, n_rows, hd), per_b),
        scratch_shapes=[pltpu.VMEM((n_rows, LANES), F32)] * 3 + [pltpu.VMEM((nblk, n_rows, hd), F32)],
    )
    return pl.pallas_call(
        kern,
        grid_spec=grid_spec,
        out_shape=jax.ShapeDtypeStruct((db, n_rows, hd), F32),
        compiler_params=_params(("parallel", "arbitrary")),
    )(page_table, q_rows, *([cache_k] * len(page_specs)), *([cache_v] * len(page_specs)), knew, vnew)


def _rope_tables(pos):
    half = ATT_HEAD_DIM // 2
    inv_freq = jnp.exp(-math.log(ROPE_THETA) * jnp.arange(half, dtype=F32) * 2.0 / ATT_HEAD_DIM)
    ang = pos[:, None] * inv_freq[None, :]
    cos, sin = jnp.cos(ang), jnp.sin(ang)
    return jnp.concatenate([cos, cos], axis=-1), jnp.concatenate([-sin, sin], axis=-1)


def _prep_weights(d_model, w_in, w_ssd_out, w_conv_out, w_att_out, w_o, w_up, w_down, conv_ch, n_heads, kv_heads):
    d_mix = d_model // 2
    ssd_heads = d_mix // SSD_HEADDIM
    xbc = d_mix + 2 * SSD_GROUPS * SSD_STATE
    o_dt = d_mix + xbc
    o_conv = o_dt + ssd_heads
    qkv_w = (n_heads + 2 * kv_heads) * ATT_HEAD_DIM
    wt = jnp.swapaxes(w_in, 1, 2)
    return {
        "wt": wt, "o_dt": o_dt, "o_val": o_conv, "o_gate": o_conv + conv_ch, "o_qkv": o_conv + 2 * conv_ch,
        "o_g": o_conv + 2 * conv_ch + qkv_w, "n_qkv": qkv_w, "n_g": w_in.shape[2] - (o_conv + 2 * conv_ch + qkv_w),
        "w_branch": jnp.stack([w_ssd_out, w_conv_out, w_att_out]).astype(BF16),
        "w_o": w_o, "w_up": w_up, "w_down": w_down,
    }


def _prep_layer(l, d_mix, ssd_conv_w, ssd_conv_b, ssd_dt_bias, ssd_a_log, ssd_d, ssd_norm,
                conv_w, conv_b, conv_ln_g, conv_ln_b, norm_mix, norm_ffn):
    xbc = ssd_conv_w.shape[2]
    conv_ch = conv_w.shape[2]
    pad_row = lambda v: jnp.pad(v, (0, LANES - v.shape[0])).reshape(1, LANES)
    return {
        "d_mix": d_mix,
        "norm_mix": norm_mix[l], "norm_ffn": norm_ffn[l],
        "ssd_conv_w": ssd_conv_w[l], "ssd_conv_b": ssd_conv_b[l].reshape(1, xbc),
        "ssd_dt_bias": pad_row(ssd_dt_bias[l]), "ssd_a_log": pad_row(ssd_a_log[l]),
        "ssd_d_full": jnp.repeat(ssd_d[l], SSD_HEADDIM).reshape(1, d_mix),
        "ssd_norm": ssd_norm[l].reshape(1, d_mix),
        "conv_w": conv_w[l], "conv_b": conv_b[l].reshape(1, conv_ch),
        "conv_ln_g": conv_ln_g[l].reshape(1, conv_ch), "conv_ln_b": conv_ln_b[l].reshape(1, conv_ch),
    }


def _last_rows(a, nb, t_len, n, col0=0):
    return jnp.stack([a[(b + 1) * t_len - n:(b + 1) * t_len, col0:] for b in range(nb)])


def _layer(x, p, w, l, cos_t, sin_t, cache_k, cache_v, page_table, state_ssm_l, state_ssm_conv_l, state_conv_l,
           *, nbp, t_len, dbs, dt_len, n_heads, kv_heads):
    m, d_model = x.shape
    d_mix = p["d_mix"]
    hd = ATT_HEAD_DIM
    bt = nbp * t_len
    xbc_w = p["ssd_conv_w"].shape[1]
    conv_ch = p["conv_w"].shape[1]
    kv_w = kv_heads * hd

    h = _rmsnorm(x, p["norm_mix"], BF16)
    zx = _matmul_t(h, w["wt"], l, 0, w["o_dt"])
    dtr = _matmul_t(h, w["wt"], l, w["o_dt"], LANES, tn=LANES)
    u = _matmul_glu_t(h, w["wt"], l, w["o_val"], w["o_gate"], conv_ch)
    n_rope_tiles = (n_heads + kv_heads) * hd // 512
    qkv = _matmul_t(h, w["wt"], l, w["o_qkv"], w["n_qkv"],
                    epilogue=functools.partial(_ep_rope, n_rope_tiles=n_rope_tiles),
                    extra=(cos_t, sin_t), extra_specs=(_spec_row_table, _spec_row_table), split_rows=True)
    gates = _matmul_t(h, w["wt"], l, w["o_g"], w["n_g"], epilogue=_ep_sigmoid, out_dtype=BF16, split_rows=True)

    tc = 128
    ncp = t_len // tc
    y_p, ssm_p = _ssd(zx, dtr, p, jnp.zeros((nbp, SUBLANES, xbc_w), F32),
                      jnp.zeros((nbp, d_mix, SSD_STATE), F32),
                      nb=nbp, nchunk=ncp, tc=tc, t_valid=tc, row_block0=0)
    zx_s = zx[bt:].reshape(dbs, dt_len, -1)
    pad_t = ((0, 0), (0, tc - dt_len), (0, 0))
    zx_sp = jnp.pad(zx_s, pad_t).reshape(dbs * tc, -1)
    dtr_sp = jnp.pad(dtr[bt:].reshape(dbs, dt_len, LANES), pad_t).reshape(dbs * tc, LANES)
    kc = state_ssm_conv_l.shape[1]
    buf0_s = jnp.pad(state_ssm_conv_l, ((0, 0), (SUBLANES - kc, 0), (0, 0)))
    y_s, ssm_s = _ssd(zx_sp, dtr_sp, p, buf0_s, state_ssm_l.reshape(dbs, d_mix, SSD_STATE),
                      nb=dbs, nchunk=1, tc=tc, t_valid=dt_len, row_block0=0)
    y_s = y_s.reshape(dbs, tc, d_mix)[:, :dt_len].reshape(dbs * dt_len, d_mix)
    ssm_buf_p = _last_rows(zx, nbp, t_len, kc, d_mix)
    ssm_buf_s = jnp.concatenate([state_ssm_conv_l, zx_s[:, :, d_mix:]], axis=1)[:, -kc:]

    tt = 256
    cw1 = p["conv_w"].shape[0] - 1
    c_p = _conv(u, p, jnp.zeros((nbp, CONV_HALO, conv_ch), F32), nb=nbp, ntile=t_len // tt, tt=tt, row_block0=0)
    buf0_c = jnp.pad(state_conv_l, ((0, 0), (CONV_HALO - cw1, 0), (0, 0)))
    c_s = _conv(u, p, buf0_c, nb=dbs, ntile=1, tt=dt_len, row_block0=bt // dt_len)
    conv_buf_p = _last_rows(u, nbp, t_len, cw1)
    conv_buf_s = jnp.concatenate([state_conv_l, u[bt:].reshape(dbs, dt_len, conv_ch)], axis=1)[:, -cw1:]

    a_p = _moba_prompt(qkv, nb=nbp, t_len=t_len, n_heads=n_heads, kv_heads=kv_heads)
    qkv_s = qkv[bt:]
    q_rows = qkv_s[:, :n_heads * hd].reshape(dbs, dt_len, n_heads, hd).transpose(0, 2, 1, 3)
    q_rows = q_rows.reshape(dbs, n_heads * dt_len, hd)
    k_s = qkv_s[:, n_heads * hd:n_heads * hd + kv_w].reshape(dbs, dt_len, kv_w)
    v_s = qkv_s[:, n_heads * hd + kv_w:].reshape(dbs, dt_len, kv_w)
    pad_k = ((0, 0), (0, LANES - dt_len), (0, 0))
    a_s = _moba_sample(q_rows, jnp.pad(k_s, pad_k), jnp.pad(v_s, pad_k), cache_k, cache_v, page_table, l,
                       kv_heads=kv_heads, dt_len=dt_len)
    a_s = a_s.reshape(dbs, n_heads, dt_len, hd).transpose(0, 2, 1, 3).reshape(dbs * dt_len, n_heads * hd)
    k_p = qkv[:bt, n_heads * hd:n_heads * hd + kv_w].reshape(nbp, t_len, kv_heads, hd)
    v_p = qkv[:bt, n_heads * hd + kv_w:].reshape(nbp, t_len, kv_heads, hd)

    y3 = jnp.stack([jnp.concatenate([y_p, y_s]), jnp.concatenate([c_p, c_s]),
                    jnp.concatenate([a_p, a_s.astype(BF16)])])
    merged = _merge(y3, w["w_branch"], gates, l)
    res_spec = (_spec_out_tile,)
    x = _matmul(merged, w["w_o"], l, epilogue=_ep_residual, extra=(x,), extra_specs=res_spec)
    hf = _rmsnorm(x, p["norm_ffn"], BF16)
    act = _matmul(hf, w["w_up"], l, epilogue=_ep_relu2, out_dtype=BF16)
    x = _matmul(act, w["w_down"], l, epilogue=_ep_residual, extra=(x,), extra_specs=res_spec, tk=4096)
    states = (k_p, v_p, k_s.reshape(dbs, dt_len, kv_heads, hd), v_s.reshape(dbs, dt_len, kv_heads, hd),
              ssm_p.reshape(nbp, -1, SSD_HEADDIM, SSD_STATE), ssm_s.reshape(dbs, -1, SSD_HEADDIM, SSD_STATE),
              ssm_buf_p, ssm_buf_s, conv_buf_p, conv_buf_s)
    return x, states


def kernel(x_prompt, x_sample, cache_k, cache_v, page_table, state_ssm, state_ssm_conv, state_conv, norm_mix, w_in, ssd_conv_w, ssd_conv_b, ssd_dt_bias, ssd_a_log, ssd_d, ssd_norm, w_ssd_out, conv_w, conv_b, conv_ln_g, conv_ln_b, w_conv_out, w_att_out, w_o, norm_ffn, w_up, w_down, norm_final):
    nbp, t_len, d_model = x_prompt.shape
    dbs, dt_len, _ = x_sample.shape
    depth = w_in.shape[0]
    n_phys, page, kv_heads, hd = cache_k.shape[1:]
    assert hd == ATT_HEAD_DIM
    n_heads = w_att_out.shape[1] // hd
    past_len = page_table.shape[1] * page
    assert past_len % MOBA_BLOCK == 0 and dt_len <= SUBLANES
    bt = nbp * t_len

    x = jnp.concatenate([x_prompt.reshape(bt, d_model), x_sample.reshape(dbs * dt_len, d_model)], axis=0)
    pos = jnp.concatenate([jnp.tile(jnp.arange(t_len, dtype=F32), nbp),
                           jnp.tile(past_len + jnp.arange(dt_len, dtype=F32), dbs)])
    cos_t, sin_t = _rope_tables(pos)
    ck = cache_k.reshape(depth, n_phys, page * kv_heads, hd)
    cv = cache_v.reshape(depth, n_phys, page * kv_heads, hd)
    w = _prep_weights(d_model, w_in, w_ssd_out, w_conv_out, w_att_out, w_o, w_up, w_down, conv_w.shape[2],
                      n_heads, kv_heads)

    per_layer = []
    for l in range(depth):
        p = _prep_layer(l, d_model // 2, ssd_conv_w, ssd_conv_b, ssd_dt_bias, ssd_a_log, ssd_d, ssd_norm,
                        conv_w, conv_b, conv_ln_g, conv_ln_b, norm_mix, norm_ffn)
        x, st = _layer(x, p, w, l, cos_t, sin_t, ck, cv, page_table, state_ssm[l], state_ssm_conv[l],
                       state_conv[l], nbp=nbp, t_len=t_len, dbs=dbs, dt_len=dt_len, n_heads=n_heads,
                       kv_heads=kv_heads)
        per_layer.append(st)
    y_prompt = _rmsnorm(x, norm_final, F32, 0, bt).reshape(nbp, t_len, d_model)
    y_sample = _rmsnorm(x, norm_final, F32, bt, dbs * dt_len).reshape(dbs, dt_len, d_model)
    stacked = [jnp.stack([st[i] for st in per_layer]) for i in range(10)]
    return (y_prompt, y_sample, *stacked)
```
